```python
import math
import jax
import jax.numpy as jnp
from jax import lax
import numpy as np

D_MODEL = 1024
BATCH = 8
SEQ = 4096
DEPTH = 4

GRID_W = 64
CTX_LEN = 256
N_BRANCH = 4
BRANCH_W = 256
CHUNK = 64
DN_HEADS = 4
DN_DK = 64
DN_DV = 64
DN_CONV = 3
NA_HEADS = 4
NA_DH = 64
NA_WIN_ROWS = 8
NA_WIN_COLS = 16
GLA_HEADS = 4
GLA_DK = 32
GLA_DV = 64
GLA_RANK = 16
GLA_TAU = 16.0
DIFF_HEADS = 4
DIFF_DQK = 32
DIFF_DV = 64
Q_BLOCK = 128
ROPE_THETA = 10000.0
D_FF = 2816
FFN_CONV = 3
EPS = 1e-6

DN_QKV = 2 * DN_HEADS * DN_DK + DN_HEADS * DN_DV
NA_QKV = 3 * NA_HEADS * NA_DH
DIFF_QK = DIFF_HEADS * 2 * DIFF_DQK
DIFF_QKV = 2 * DIFF_QK + DIFF_HEADS * DIFF_DV
IN_WIDTHS = (DN_QKV, DN_HEADS * DN_DV, 2 * DN_HEADS, 2 * DN_HEADS,
             NA_QKV,
             GLA_HEADS * GLA_DK, GLA_HEADS * GLA_DK, GLA_HEADS * GLA_DV, GLA_HEADS * GLA_DV, 2 * GLA_RANK,
             DIFF_QKV,
             N_BRANCH * D_MODEL)
N_IN = sum(IN_WIDTHS)

kernel_name = 'hybrid_parallel_mixer_dit_block'


def rms_norm(x, g):
    xf = x.astype(jnp.float32)
    y = xf * lax.rsqrt(jnp.mean(xf * xf, axis=-1, keepdims=True) + EPS)
    return (y * g.astype(jnp.float32)).astype(x.dtype)


def l2_normalize(x):
    xf = x.astype(jnp.float32)
    return xf * lax.rsqrt(jnp.sum(xf * xf, axis=-1, keepdims=True) + EPS)


def dwconv_centred(x, w):
    width, ch = w.shape
    pad = width // 2
    return lax.conv_general_dilated(x, w[:, None, :].astype(x.dtype), window_strides=(1,),
                                    padding=[(pad, pad)], dimension_numbers=('NWC', 'WIO', 'NWC'),
                                    feature_group_count=ch)


def axial_rope_tables(length):
    t = jnp.arange(length)
    row = (t // GRID_W).astype(jnp.float32)
    col = (t % GRID_W).astype(jnp.float32)
    n_freq = DIFF_DQK // 4
    inv_freq = jnp.power(jnp.float32(ROPE_THETA), -jnp.arange(n_freq, dtype=jnp.float32) / n_freq)
    ang_r = row[:, None] * inv_freq
    ang_c = col[:, None] * inv_freq
    ang = jnp.concatenate([ang_r, ang_r, ang_c, ang_c], axis=-1)
    return jnp.cos(ang), jnp.sin(ang)


def apply_axial_rope(x, cos, sin):
    a, b, c, d = jnp.split(x, 4, axis=-1)
    rotated = jnp.concatenate([-b, a, -d, c], axis=-1)
    return x * cos.astype(x.dtype) + rotated * sin.astype(x.dtype)


def _flip_seq(t):
    return jnp.flip(t, axis=2)


def gated_delta_chunked(q, k, v, g, beta, s0):
    f32 = jnp.float32
    B, H, L, dk = q.shape
    dv = v.shape[-1]
    n, C = L // CHUNK, CHUNK
    q, k, v = [t.astype(f32).reshape(B, H, n, C, t.shape[-1]) for t in (q, k, v)]
    g = g.astype(f32).reshape(B, H, n, C)
    beta = beta.astype(f32).reshape(B, H, n, C)
    gc = jnp.cumsum(g, axis=-1)
    incl = jnp.tril(jnp.ones((C, C), dtype=bool))
    strict = jnp.tril(jnp.ones((C, C), dtype=bool), -1)
    decay = jnp.exp(jnp.where(incl, gc[..., :, None] - gc[..., None, :], -jnp.inf))
    kb = k * beta[..., None]
    n_mat = jnp.where(strict, jnp.einsum('bhncd,bhnjd->bhncj', kb, k) * decay, 0.0)
    rhs = jnp.concatenate([v * beta[..., None], kb * jnp.exp(gc)[..., None]], axis=-1)
    sol = lax.linalg.triangular_solve(jnp.eye(C, dtype=f32) + n_mat, rhs, left_side=True, lower=True)
    u, w = sol[..., :dv], sol[..., dv:]
    a_qk = jnp.einsum('bhncd,bhnjd->bhncj', q, k) * decay
    q_dec = q * jnp.exp(gc)[..., None]
    k_dec = k * jnp.exp(gc[..., -1:] - gc)[..., None]
    g_last = jnp.exp(gc[..., -1])

    def step(S, xs):
        u_i, w_i, a_i, qd_i, kd_i, gl_i = xs
        v_new = u_i - jnp.einsum('bhcd,bhdv->bhcv', w_i, S)
        o_i = jnp.einsum('bhcd,bhdv->bhcv', qd_i, S) + jnp.einsum('bhcj,bhjv->bhcv', a_i, v_new)
        S = S * gl_i[..., None, None] + jnp.einsum('bhcd,bhcv->bhdv', kd_i, v_new)
        return S, o_i

    xs = tuple(jnp.moveaxis(t, 2, 0) for t in (u, w, a_qk, q_dec, k_dec, g_last))
    s_fin, o = lax.scan(step, s0, xs)
    return jnp.moveaxis(o, 0, 2).reshape(B, H, L, dv), s_fin


def gla_chunked(q, k, v, log_a, s0):
    f32 = jnp.float32
    B, H, L, dk = q.shape
    dv = v.shape[-1]
    n, C = L // CHUNK, CHUNK
    q, k, v, log_a = [t.astype(f32).reshape(B, H, n, C, t.shape[-1]) for t in (q, k, v, log_a)]
    b = jnp.cumsum(log_a, axis=3)
    b_mid = b[:, :, :, C // 2 - 1:C // 2, :]
    a_intra = jnp.einsum('bhncd,bhnjd->bhncj', q * jnp.exp(b - b_mid), k * jnp.exp(b_mid - b))
    a_intra = jnp.where(jnp.tril(jnp.ones((C, C), dtype=bool)), a_intra, 0.0)
    o_intra = jnp.einsum('bhncj,bhnjv->bhncv', a_intra, v)
    b_last = b[:, :, :, -1, :]
    ds = jnp.einsum('bhncd,bhncv->bhndv', k * jnp.exp(b_last[:, :, :, None, :] - b), v)

    def step(S, xs):
        ds_i, dec_i = xs
        return dec_i[..., None] * S + ds_i, S

    s_fin, s_prev = lax.scan(step, s0, (jnp.moveaxis(ds, 2, 0), jnp.moveaxis(jnp.exp(b_last), 2, 0)))
    o_inter = jnp.einsum('bhncd,nbhdv->bhncv', q * jnp.exp(b), s_prev)
    return (o_intra + o_inter).reshape(B, H, L, dv), s_fin


def bidirectional_scan(chunk_fn, lat_shared, lat_dir, ctx_shared, ctx_dir, with_ctx):
    q_c, v_c = ctx_shared[0], ctx_shared[2]
    s0 = jnp.zeros(q_c.shape[:2] + (q_c.shape[-1], v_c.shape[-1]), jnp.float32)
    outs_l, outs_c = [], []
    for d in range(2):
        orient = _flip_seq if d == 1 else (lambda t: t)
        c_args = [orient(t) for t in ctx_shared] + [orient(t[d]) for t in ctx_dir]
        l_args = [orient(t) for t in lat_shared] + [orient(t[d]) for t in lat_dir]
        o_c, s_ctx = chunk_fn(*c_args, s0)
        o_l, _ = chunk_fn(*l_args, s_ctx)
        outs_l.append(orient(o_l))
        outs_c.append(orient(o_c))
    y_l = outs_l[0] + outs_l[1]
    y_c = outs_c[0] + outs_c[1] if with_ctx else None
    return y_l, y_c


def _gated_head_norm(o, gate, g):
    B, H, L, dv = o.shape
    on = rms_norm(jnp.transpose(o, (0, 2, 1, 3)), g)
    out = on * jax.nn.silu(gate.astype(jnp.float32)).reshape(B, L, H, dv)
    return out.reshape(B, L, H * dv).astype(gate.dtype)


def _dn_inputs(qkv, beta_raw, a_raw, conv_w, a_log, dt_bias):
    B, L, _ = qkv.shape
    q, k, v = jnp.split(jax.nn.silu(dwconv_centred(qkv, conv_w)), [DN_HEADS * DN_DK, 2 * DN_HEADS * DN_DK], axis=-1)
    heads = lambda t, d: jnp.transpose(t.reshape(B, L, DN_HEADS, d), (0, 2, 1, 3))
    q = l2_normalize(heads(q, DN_DK)) * DN_DK ** -0.5
    k = l2_normalize(heads(k, DN_DK))
    beta = jax.nn.sigmoid(beta_raw.astype(jnp.float32)).reshape(B, L, 2, DN_HEADS)
    g = -jnp.exp(a_log.astype(jnp.float32)) * jax.nn.softplus(
        a_raw.astype(jnp.float32).reshape(B, L, 2, DN_HEADS) + dt_bias.astype(jnp.float32))
    per_dir = lambda t: jnp.transpose(t, (2, 0, 3, 1))
    return (q, k, heads(v, DN_DV)), (per_dir(g), per_dir(beta))


def mixer_gated_deltanet(lat, ctx, conv_w, a_log, dt_bias, norm_g, with_ctx):
    ls, ld = _dn_inputs(lat[0], lat[2], lat[3], conv_w, a_log, dt_bias)
    cs, cd = _dn_inputs(ctx[0], ctx[2], ctx[3], conv_w, a_log, dt_bias)
    o_l, o_c = bidirectional_scan(gated_delta_chunked, ls, ld, cs, cd, with_ctx)
    y_l = _gated_head_norm(o_l, lat[1], norm_g)
    y_c = _gated_head_norm(o_c, ctx[1], norm_g) if with_ctx else None
    return y_l, y_c


def _gla_inputs(q, k, v, a1, w_a2, b_a):
    B, L, _ = q.shape
    heads = lambda t, d: jnp.transpose(t.reshape(B, L, GLA_HEADS, d), (0, 2, 1, 3))
    logit = jnp.einsum('blnr,nrk->blnk', a1.reshape(B, L, 2, GLA_RANK), w_a2) + b_a
    log_a = jax.nn.log_sigmoid(logit.astype(jnp.float32)) / GLA_TAU
    log_a = jnp.transpose(log_a.reshape(B, L, 2, GLA_HEADS, GLA_DK), (2, 0, 3, 1, 4))
    return (heads(q, GLA_DK) * GLA_DK ** -0.5, heads(k, GLA_DK), heads(v, GLA_DV)), (log_a,)


def mixer_gla(lat, ctx, w_a2, b_a, norm_g, with_ctx):
    ls, ld = _gla_inputs(lat[0], lat[1], lat[2], lat[4], w_a2, b_a)
    cs, cd = _gla_inputs(ctx[0], ctx[1], ctx[2], ctx[4], w_a2, b_a)
    o_l, o_c = bidirectional_scan(gla_chunked, ls, ld, cs, cd, with_ctx)
    y_l = _gated_head_norm(o_l, lat[3], norm_g)
    y_c = _gated_head_norm(o_c, ctx[3], norm_g) if with_ctx else None
    return y_l, y_c


def mixer_neighbourhood(qkv_l, qkv_c, q_norm, k_norm, rpb, with_ctx):
    def prep(qkv):
        B, L, _ = qkv.shape
        t = qkv.reshape(B, L, 3, NA_HEADS, NA_DH)
        return rms_norm(t[:, :, 0], q_norm), rms_norm(t[:, :, 1], k_norm), t[:, :, 2]

    ql, kl, vl = prep(qkv_l)
    qc, kc, vc = prep(qkv_c)
    B, L = ql.shape[:2]
    rows = L // GRID_W
    wr = min(NA_WIN_ROWS, rows)
    wc = NA_WIN_COLS
    scale = NA_DH ** -0.5
    row_idx = jnp.arange(rows)
    row_start = jnp.clip(row_idx - wr // 2, 0, rows - wr)
    col = jnp.arange(GRID_W)
    col_keys = jnp.clip(col - wc // 2, 0, GRID_W - wc)[:, None] + jnp.arange(wc)
    col_bias = col_keys - col[:, None] + (NA_WIN_COLS - 1)
    k_grid = kl.reshape(B, rows, GRID_W, NA_HEADS, NA_DH)
    v_grid = vl.reshape(B, rows, GRID_W, NA_HEADS, NA_DH)
    q_rows = jnp.moveaxis(ql.reshape(B, rows, GRID_W, NA_HEADS, NA_DH), 1, 0)

    def row_block(xs):
        q_r, rs, r = xs
        k_win = lax.dynamic_slice_in_dim(k_grid, rs, wr, axis=1)[:, :, col_keys]
        v_win = lax.dynamic_slice_in_dim(v_grid, rs, wr, axis=1)[:, :, col_keys]
        r_bias = rs + jnp.arange(wr) - r + (NA_WIN_ROWS - 1)
        bias = jnp.transpose(rpb[:, r_bias[:, None, None], col_bias[None, :, :]], (0, 2, 1, 3))
        s_win = jnp.einsum('bqhd,brqjhd->bhqrj', q_r, k_win).astype(jnp.float32) * scale + bias.astype(jnp.float32)
        s_ctx = jnp.einsum('bqhd,bkhd->bhqk', q_r, kc).astype(jnp.float32) * scale
        p = jax.nn.softmax(jnp.concatenate([s_win.reshape(B, NA_HEADS, GRID_W, wr * wc), s_ctx], axis=-1), axis=-1)
        p = p.astype(vl.dtype)
        p_win = p[..., :wr * wc].reshape(B, NA_HEADS, GRID_W, wr, wc)
        return (jnp.einsum('bhqrj,brqjhd->bqhd', p_win, v_win)
                + jnp.einsum('bhqk,bkhd->bqhd', p[..., wr * wc:], vc))

    o = lax.map(row_block, (q_rows, row_start, row_idx))
    y_l = jnp.moveaxis(o, 0, 1).reshape(B, L, NA_HEADS * NA_DH)
    y_c = None
    if with_ctx:
        s = jnp.einsum('bqhd,bkhd->bhqk', qc, kc).astype(jnp.float32) * scale
        p = jax.nn.softmax(s, axis=-1).astype(vc.dtype)
        y_c = jnp.einsum('bhqk,bkhd->bqhd', p, vc).reshape(B, qc.shape[1], NA_HEADS * NA_DH)
    return y_l, y_c


def mixer_differential(qkv_l, qkv_c, q_norm, k_norm, lam_params, norm_g, lam_init, cos, sin, with_ctx):
    def prep(qkv):
        B, L, _ = qkv.shape
        q = rms_norm(qkv[..., :DIFF_QK].reshape(B, L, DIFF_HEADS, 2, DIFF_DQK), q_norm)
        k = rms_norm(qkv[..., DIFF_QK:2 * DIFF_QK].reshape(B, L, DIFF_HEADS, 2, DIFF_DQK), k_norm)
        v = qkv[..., 2 * DIFF_QK:].reshape(B, L, DIFF_HEADS, DIFF_DV)
        return q, k, v

    ql, kl, vl = prep(qkv_l)
    qc, kc, vc = prep(qkv_c)
    B, L = ql.shape[:2]
    c5, s5 = cos[None, :, None, None, :], sin[None, :, None, None, :]
    ql = apply_axial_rope(ql, c5, s5)
    kl = apply_axial_rope(kl, c5, s5)
    lp = lam_params.astype(jnp.float32)
    lam = jnp.exp(jnp.sum(lp[0] * lp[1])) - jnp.exp(jnp.sum(lp[2] * lp[3])) + lam_init
    k_all = jnp.concatenate([kl, kc], axis=1)
    v_all = jnp.concatenate([vl, vc], axis=1)
    scale = DIFF_DQK ** -0.5

    def attend(qb, keys, vals):
        s = jnp.einsum('bqhtd,bkhtd->bhtqk', qb, keys).astype(jnp.float32) * scale
        p = jax.nn.softmax(s, axis=-1)
        w = p[:, :, 0] - lam * p[:, :, 1]
        return jnp.einsum('bhqk,bkhd->bqhd', w.astype(vals.dtype), vals)

    def post(o):
        return (rms_norm(o, norm_g) * (1.0 - lam_init)).reshape(o.shape[0], o.shape[1], DIFF_HEADS * DIFF_DV)

    nb = L // Q_BLOCK
    q_blocks = jnp.moveaxis(ql.reshape(B, nb, Q_BLOCK, DIFF_HEADS, 2, DIFF_DQK), 1, 0)
    o_l = lax.map(lambda qb: attend(qb, k_all, v_all), q_blocks)
    y_l = post(jnp.moveaxis(o_l, 0, 1).reshape(B, L, DIFF_HEADS, DIFF_DV))
    y_c = post(attend(qc, kc, vc)) if with_ctx else None
    return y_l, y_c


def merge_branches(branches, gate_raw, b_gate, w_branch, w_out):
    y = jnp.stack(branches, axis=2)
    B, L = y.shape[:2]
    proj = jnp.einsum('blgw,gwd->blgd', y, w_branch)
    gate = jax.nn.sigmoid(gate_raw.reshape(B, L, N_BRANCH, D_MODEL) + b_gate)
    return jnp.sum(gate * proj, axis=2) @ w_out


def conv_ffn(h, w_in, conv_w, conv_b, w_out):
    u, v = jnp.split(h @ w_in, 2, axis=-1)
    u = dwconv_centred(u, conv_w) + conv_b
    return (jax.nn.silu(u) * v) @ w_out


def setup_inputs(seed: int = 0) -> dict:
    key = jax.random.key(seed)
    ks = jax.random.split(key, 32)
    f32 = jnp.float32

    def nrm(i, shape, scale):
        return jax.random.normal(ks[i], shape, f32) * scale

    dt = jnp.exp(jax.random.uniform(ks[10], (DEPTH, 2, DN_HEADS), f32, minval=math.log(1e-3), maxval=math.log(1e-1)))
    return {
        'x': nrm(0, (BATCH, SEQ, D_MODEL), 1.0),
        'c': nrm(1, (BATCH, D_MODEL), 1.0),
        'ctx': nrm(2, (BATCH, CTX_LEN, D_MODEL), 1.0),
        'c_ctx': nrm(3, (D_MODEL,), 1.0),
        'w_mod': nrm(4, (DEPTH, D_MODEL, 6 * D_MODEL), 0.5 * D_MODEL ** -0.5),
        'b_mod': nrm(5, (DEPTH, 6 * D_MODEL), 0.02),
        'norm1_g': 1.0 + nrm(6, (DEPTH, D_MODEL), 0.02),
        'norm2_g': 1.0 + nrm(7, (DEPTH, D_MODEL), 0.02),
        'w_in': nrm(8, (DEPTH, D_MODEL, N_IN), D_MODEL ** -0.5),
        'b_gate': nrm(9, (DEPTH, N_BRANCH, D_MODEL), 0.02),
        'dn_conv': nrm(11, (DEPTH, DN_CONV, DN_QKV), DN_CONV ** -0.5),
        'dn_a_log': jnp.log(jax.random.uniform(ks[12], (DEPTH, 2, DN_HEADS), f32, minval=1.0, maxval=16.0)),
        'dn_dt_bias': dt + jnp.log(-jnp.expm1(-dt)),
        'dn_norm_g': 1.0 + nrm(13, (DEPTH, DN_DV), 0.02),
        'na_q_norm': 1.0 + nrm(14, (DEPTH, NA_DH), 0.02),
        'na_k_norm': 1.0 + nrm(15, (DEPTH, NA_DH), 0.02),
        'na_rpb': nrm(16, (DEPTH, NA_HEADS, 2 * NA_WIN_ROWS - 1, 2 * NA_WIN_COLS - 1), 0.1),
        'gla_w_a2': nrm(17, (DEPTH, 2, GLA_RANK, GLA_HEADS * GLA_DK), GLA_RANK ** -0.5),
        'gla_b_a': nrm(18, (DEPTH, 2, GLA_HEADS * GLA_DK), 0.1),
        'gla_norm_g': 1.0 + nrm(19, (DEPTH, GLA_DV), 0.02),
        'df_q_norm': 1.0 + nrm(20, (DEPTH, DIFF_DQK), 0.02),
        'df_k_norm': 1.0 + nrm(21, (DEPTH, DIFF_DQK), 0.02),
        'df_lambda': nrm(22, (DEPTH, 4, DIFF_DQK), 0.1),
        'df_norm_g': 1.0 + nrm(23, (DEPTH, DIFF_DV), 0.02),
        'w_branch': nrm(24, (DEPTH, N_BRANCH, BRANCH_W, D_MODEL), BRANCH_W ** -0.5),
        'w_out': nrm(25, (DEPTH, D_MODEL, D_MODEL), D_MODEL ** -0.5),
        'ffn_w_in': nrm(26, (DEPTH, D_MODEL, 2 * D_FF), D_MODEL ** -0.5),
        'ffn_conv_w': nrm(27, (DEPTH, FFN_CONV, D_FF), FFN_CONV ** -0.5),
        'ffn_conv_b': nrm(28, (DEPTH, D_FF), 0.02),
        'ffn_w_out': nrm(29, (DEPTH, D_FF, D_MODEL), D_FF ** -0.5),
    }


def reference(x, c, ctx, c_ctx, w_mod, b_mod, norm1_g, norm2_g, w_in, b_gate,
              dn_conv, dn_a_log, dn_dt_bias, dn_norm_g,
              na_q_norm, na_k_norm, na_rpb,
              gla_w_a2, gla_b_a, gla_norm_g,
              df_q_norm, df_k_norm, df_lambda, df_norm_g,
              w_branch, w_out, ffn_w_in, ffn_conv_w, ffn_conv_b, ffn_w_out):
    L = x.shape[1]
    cos, sin = axial_rope_tables(L)
    split_at = np.cumsum(IN_WIDTHS)[:-1].tolist()
    xc = ctx
    for li in range(DEPTH):
        with_ctx = li < DEPTH - 1
        lam_init = 0.8 - 0.6 * math.exp(-0.3 * li)
        mod = jax.nn.silu(c) @ w_mod[li] + b_mod[li]
        mod_c = jax.nn.silu(c_ctx) @ w_mod[li] + b_mod[li]
        sh1, sc1, g1, sh2, sc2, g2 = jnp.split(mod[:, None, :], 6, axis=-1)
        csh1, csc1, cg1, csh2, csc2, cg2 = jnp.split(mod_c[None, None, :], 6, axis=-1)
        h = rms_norm(x, norm1_g[li]) * (1.0 + sc1) + sh1
        hc = rms_norm(xc, norm1_g[li]) * (1.0 + csc1) + csh1
        pl = jnp.split(h @ w_in[li], split_at, axis=-1)
        pc = jnp.split(hc @ w_in[li], split_at, axis=-1)
        ya = mixer_gated_deltanet(pl[0:4], pc[0:4], dn_conv[li], dn_a_log[li], dn_dt_bias[li], dn_norm_g[li], with_ctx)
        yb = mixer_neighbourhood(pl[4], pc[4], na_q_norm[li], na_k_norm[li], na_rpb[li], with_ctx)
        yc = mixer_gla(pl[5:10], pc[5:10], gla_w_a2[li], gla_b_a[li], gla_norm_g[li], with_ctx)
        yd = mixer_differential(pl[10], pc[10], df_q_norm[li], df_k_norm[li], df_lambda[li], df_norm_g[li],
                                lam_init, cos, sin, with_ctx)
        x = x + g1 * merge_branches((ya[0], yb[0], yc[0], yd[0]), pl[11], b_gate[li], w_branch[li], w_out[li])
        h2 = rms_norm(x, norm2_g[li]) * (1.0 + sc2) + sh2
        x = x + g2 * conv_ffn(h2, ffn_w_in[li], ffn_conv_w[li], ffn_conv_b[li], ffn_w_out[li])
        if with_ctx:
            xc = xc + cg1 * merge_branches((ya[1], yb[1], yc[1], yd[1]), pc[11], b_gate[li], w_branch[li], w_out[li])
            hc2 = rms_norm(xc, norm2_g[li]) * (1.0 + csc2) + csh2
            xc = xc + cg2 * conv_ffn(hc2, ffn_w_in[li], ffn_conv_w[li], ffn_conv_b[li], ffn_w_out[li])
    return x
```

```python
import functools
import math

import numpy as np
import jax
import jax.numpy as jnp
from jax import lax
from jax.experimental import pallas as pl
from jax.experimental.pallas import tpu as pltpu

F32 = jnp.float32
BF16 = jnp.bfloat16
HI = lax.Precision.HIGHEST

D_MODEL = 1024
DEPTH = 4
GRID_W = 64
CTX_LEN = 256
N_BRANCH = 4
BRANCH_W = 256
CHUNK = 64
N_HEADS = 4
DN_DK = 64
DN_CONV = 3
NA_DH = 64
NA_WIN_ROWS = 8
NA_WIN_COLS = 16
GLA_DK = 32
GLA_RANK = 16
GLA_TAU = 16.0
DIFF_DQK = 32
ROPE_THETA = 10000.0
D_FF = 2816
EPS = 1e-6
NEG = -1e30

LANES = 128
SUBLANES = 8
VMEM_LIMIT = 56 * 1024 * 1024

C_DN = 0
C_NA = 768
C_DF = 1536
C_DNZ = 2304
C_DFROT = 2560
C_GQK = 3072
C_GV = 3328
C_GR = 3584
C_MISC = 3840
C_GATE = 4096
P_WIDTH = 8192

GROUP = 4 * CHUNK
NA_QROWS = 4
NA_KROWS = 12


def _dot(a, b, prec=None):
    return jnp.dot(a, b, preferred_element_type=F32, precision=prec)


def _dot_nt(a, b, prec=None):
    return lax.dot_general(a, b, (((1,), (1,)), ((), ())), preferred_element_type=F32, precision=prec)


def _dot_tn(a, b, prec=None):
    return lax.dot_general(a, b, (((0,), (0,)), ((), ())), preferred_element_type=F32, precision=prec)


def _iota(shape, dim):
    return lax.broadcasted_iota(jnp.int32, shape, dim)


def _group_mask(rows, cols, rgroup, cgroup):
    return (_iota((rows, cols), 0) // rgroup == _iota((rows, cols), 1) // cgroup).astype(F32)


def _group_mean(xsq, width):
    n = xsq.shape[-1]
    ones = _group_mask(n, n, width, width)
    return _dot(xsq, ones, HI) * (1.0 / width)


def _sigmoid(x):
    return 1.0 / (1.0 + jnp.exp(-x))


def _silu(x):
    return x * _sigmoid(x)


def _softplus(x):
    return jnp.maximum(x, 0.0) + jnp.log1p(jnp.exp(-jnp.abs(x)))


def _params(sem):
    return pltpu.CompilerParams(dimension_semantics=sem, vmem_limit_bytes=VMEM_LIMIT)


def _mod_kernel(c_ref, w_ref, b_ref, o_ref):
    o_ref[...] = _dot(_silu(c_ref[...]), w_ref[...], HI) + b_ref[...]


def _modulation(cc, w_mod, b_mod):
    tn = 1024
    return pl.pallas_call(
        _mod_kernel,
        out_shape=jax.ShapeDtypeStruct((DEPTH, 16, 6 * D_MODEL), F32),
        grid=(DEPTH, 6 * D_MODEL // tn),
        in_specs=[pl.BlockSpec((16, D_MODEL), lambda l, j: (0, 0)),
                  pl.BlockSpec((None, D_MODEL, tn), lambda l, j: (l, 0, j)),
                  pl.BlockSpec((None, 1, tn), lambda l, j: (l, 0, j))],
        out_specs=pl.BlockSpec((None, 16, tn), lambda l, j: (l, 0, j)),
        compiler_params=_params(("arbitrary", "arbitrary")),
        name="modulation",
    )(cc, w_mod, b_mod.reshape(DEPTH, 1, 6 * D_MODEL))


def _inproj_kernel(x_ref, g_ref, sc_ref, sh_ref, w_ref, o_ref, h_scr):
    @pl.when(pl.program_id(1) == 0)
    def _():
        x = x_ref[...]
        y = x * lax.rsqrt(jnp.mean(x * x, axis=-1, keepdims=True) + EPS)
        h_scr[...] = ((y * g_ref[...]) * sc_ref[...] + sh_ref[...]).astype(BF16)

    o_ref[...] = _dot(h_scr[...], w_ref[...])


def _inproj(x2, gain, scale1p, shift, w, rows_per_mod, tm):
    rows = x2.shape[0]
    tn = 512
    mod_spec = pl.BlockSpec((None, 1, D_MODEL), lambda i, j: ((i * tm) // rows_per_mod, 0, 0))
    return pl.pallas_call(
        _inproj_kernel,
        out_shape=jax.ShapeDtypeStruct((rows, P_WIDTH), F32),
        grid=(rows // tm, P_WIDTH // tn),
        in_specs=[pl.BlockSpec((tm, D_MODEL), lambda i, j: (i, 0)),
                  pl.BlockSpec((1, D_MODEL), lambda i, j: (0, 0)),
                  mod_spec, mod_spec,
                  pl.BlockSpec((D_MODEL, tn), lambda i, j: (0, j))],
        out_specs=pl.BlockSpec((tm, tn), lambda i, j: (i, j)),
        scratch_shapes=[pltpu.VMEM((tm, D_MODEL), BF16)],
        compiler_params=_params(("parallel", "arbitrary")),
        name="inproj",
    )(x2, gain, scale1p, shift, w)


def _prep_kernel(dn_ref, dnp_ref, dnn_ref, misc_ref, na_ref, df_ref, dfrot_ref, cos_ref, sin_ref,
                 convw_ref, alog_ref, dt_ref, naq_ref, nak_ref, dfq_ref, dfk_ref, dfqr_ref, dfkr_ref,
                 dn_o, bg_o, na_o, df_o, xe_scr, *, tp, seq_len, rope):
    i = pl.program_id(0)
    first = (i * tp) % seq_len == 0
    last = ((i + 1) * tp) % seq_len == 0
    xe_scr[0:SUBLANES, :] = jnp.where(first, 0.0, dnp_ref[...])
    xe_scr[SUBLANES:SUBLANES + tp, :] = dn_ref[...]
    xe_scr[SUBLANES + tp:2 * SUBLANES + tp, :] = jnp.where(last, 0.0, dnn_ref[...])
    cw = convw_ref[...]
    conv = (xe_scr[SUBLANES - 1:SUBLANES - 1 + tp, :] * cw[0:1]
            + xe_scr[SUBLANES:SUBLANES + tp, :] * cw[1:2]
            + xe_scr[SUBLANES + 1:SUBLANES + 1 + tp, :] * cw[2:3])
    s = _silu(conv)
    q, k, v = s[:, 0:256], s[:, 256:512], s[:, 512:768]
    qn = q * lax.rsqrt(_group_mean(q * q, DN_DK) * DN_DK + EPS) * (DN_DK ** -0.5)
    kn = k * lax.rsqrt(_group_mean(k * k, DN_DK) * DN_DK + EPS)
    dn_o[:, 0:256] = qn
    dn_o[:, 256:512] = kn
    dn_o[:, 512:768] = v
    m = misc_ref[...]
    beta = _sigmoid(m)
    g = -jnp.exp(alog_ref[...]) * _softplus(m + dt_ref[...])
    bg_o[...] = jnp.where(_iota(m.shape, 1) < 2 * N_HEADS, beta, g)
    x = na_ref[...]
    q, k, v = x[:, 0:256], x[:, 256:512], x[:, 512:768]
    qn = q * lax.rsqrt(_group_mean(q * q, NA_DH) + EPS) * naq_ref[...] * (NA_DH ** -0.5)
    kn = k * lax.rsqrt(_group_mean(k * k, NA_DH) + EPS) * nak_ref[...]
    na_o[:, 0:256] = qn.astype(BF16)
    na_o[:, 256:512] = kn.astype(BF16)
    na_o[:, 512:768] = v.astype(BF16)
    x = df_ref[...]
    q, k, v = x[:, 0:256], x[:, 256:512], x[:, 512:768]
    rq = lax.rsqrt(_group_mean(q * q, DIFF_DQK) + EPS)
    rk = lax.rsqrt(_group_mean(k * k, DIFF_DQK) + EPS)
    qn = q * rq * dfq_ref[...]
    kn = k * rk * dfk_ref[...]
    if rope:
        xr = dfrot_ref[...]
        cos, sin = cos_ref[...], sin_ref[...]
        qn = qn * cos + (xr[:, 0:256] * rq * dfqr_ref[...]) * sin
        kn = kn * cos + (xr[:, 256:512] * rk * dfkr_ref[...]) * sin
    df_o[:, 0:256] = (qn * (DIFF_DQK ** -0.5)).astype(BF16)
    df_o[:, 256:512] = kn.astype(BF16)
    df_o[:, 512:768] = v.astype(BF16)


def _prep(p, cos, sin, vecs, *, tp, seq_len, rope):
    rows = p.shape[0]
    nb8 = rows // SUBLANES
    t8 = tp // SUBLANES
    nseq_tiles = max(seq_len // tp, 1)
    row_vec = lambda w: pl.BlockSpec((1, w), lambda i: (0, 0))
    kern = functools.partial(_prep_kernel, tp=tp, seq_len=seq_len, rope=rope)
    return pl.pallas_call(
        kern,
        out_shape=(jax.ShapeDtypeStruct((rows, 768), F32), jax.ShapeDtypeStruct((rows, LANES), F32),
                   jax.ShapeDtypeStruct((rows, 768), BF16), jax.ShapeDtypeStruct((rows, 768), BF16)),
        grid=(rows // tp,),
        in_specs=[pl.BlockSpec((tp, 768), lambda i: (i, C_DN // 768)),
                  pl.BlockSpec((SUBLANES, 768), lambda i: (jnp.maximum(i * t8 - 1, 0), C_DN // 768)),
                  pl.BlockSpec((SUBLANES, 768), lambda i: (jnp.minimum((i + 1) * t8, nb8 - 1), C_DN // 768)),
                  pl.BlockSpec((tp, LANES), lambda i: (i, C_MISC // LANES)),
                  pl.BlockSpec((tp, 768), lambda i: (i, C_NA // 768)),
                  pl.BlockSpec((tp, 768), lambda i: (i, C_DF // 768)),
                  pl.BlockSpec((tp, 512), lambda i: (i, C_DFROT // 512)),
                  pl.BlockSpec((tp, 256), lambda i: (i % nseq_tiles, 0)),
                  pl.BlockSpec((tp, 256), lambda i: (i % nseq_tiles, 0)),
                  pl.BlockSpec((DN_CONV, 768), lambda i: (0, 0)),
                  row_vec(LANES), row_vec(LANES),
                  row_vec(256), row_vec(256), row_vec(256), row_vec(256), row_vec(256), row_vec(256)],
        out_specs=(pl.BlockSpec((tp, 768), lambda i: (i, 0)), pl.BlockSpec((tp, LANES), lambda i: (i, 0)),
                   pl.BlockSpec((tp, 768), lambda i: (i, 0)), pl.BlockSpec((tp, 768), lambda i: (i, 0))),
        scratch_shapes=[pltpu.VMEM((tp + 2 * SUBLANES, 768), F32)],
        compiler_params=_params(("parallel",)),
        name="prep",
    )(p, p, p, p, p, p, p, cos, sin, *vecs)


def _mm(a, b):
    return _dot(a.astype(BF16), b.astype(BF16))


def _mm_nt(a, b):
    return _dot_nt(a.astype(BF16), b.astype(BF16))


def _mm_tn(a, b):
    return _dot_tn(a.astype(BF16), b.astype(BF16))


def _tile4(x):
    return jnp.concatenate([x, x, x, x], axis=0)


def _tri_consts(d, width):
    rows = _iota((CHUNK, width), 0)
    lane_j = _iota((CHUNK, width), 1) % CHUNK
    incl = (lane_j <= rows) if d == 0 else (lane_j >= rows)
    strict = (lane_j < rows) if d == 0 else (lane_j > rows)
    eye_cat = (lane_j == rows).astype(F32)
    r64 = _iota((CHUNK, CHUNK), 0)
    c64 = _iota((CHUNK, CHUNK), 1)
    cum_mat = ((c64 <= r64) if d == 0 else (c64 >= r64)).astype(F32)
    return incl, strict, eye_cat, cum_mat


def _dn_chunk(q, k, v, bg, s_bd, d):
    incl, strict, eye_cat, cum_mat = _tri_consts(d, 256)
    mask_bd = _group_mask(256, 256, CHUNK, CHUNK)
    srow = _iota((LANES, 512), 0)
    scol = _iota((LANES, 512), 1)
    sel = (srow == (scol // 256) * 2 * N_HEADS + d * N_HEADS + (scol % 256) // CHUNK).astype(F32)
    e = _dot(bg, sel, HI)
    beta, g = e[:, 0:256], e[:, 256:512]
    gc = _dot(cum_mat, g, HI)
    gc_row = jnp.sum(eye_cat * gc, axis=0, keepdims=True)
    decay = jnp.exp(jnp.where(incl, gc - gc_row, -jnp.inf))
    gc_last = gc[CHUNK - 1:CHUNK] if d == 0 else gc[0:1]

    def bd(x):
        return _tile4(x) * mask_bd

    kb = k * beta
    gram = _mm_nt(jnp.concatenate([kb, q], axis=0), bd(k))
    n_cat = jnp.where(strict, gram[0:CHUNK] * decay, 0.0)
    a_cat = jnp.where(incl, gram[CHUNK:2 * CHUNK] * decay, 0.0)
    rows = _iota((CHUNK, 256), 0)
    lane_j = _iota((CHUNK, 256), 1) % CHUNK
    p = eye_cat - jnp.where(rows // 2 == lane_j // 2, n_cat, 0.0)
    size = 2
    while size < CHUNK:
        off = (rows // (2 * size) == lane_j // (2 * size)) & (rows // size != lane_j // size)
        p = p - _mm(_mm(p, bd(jnp.where(off, n_cat, 0.0))), bd(p))
        size *= 2
    egc = jnp.exp(gc)
    u = _mm(p, bd(v * beta))
    w = _mm(p, bd(kb * egc))
    v_new = u - _mm(w, s_bd)
    o = _mm(q * egc, s_bd) + _mm(a_cat, bd(v_new))
    kd = k * jnp.exp(gc_last - gc)
    s_new = s_bd * jnp.exp(gc_last) + mask_bd * _mm_tn(kd, v_new)
    return o, s_new


def _dn_group(qkv_f, bg_f, qkv_b, bg_b, of_ref, ob_ref, sf_scr, sb_scr):
    n = qkv_f.shape[0] // CHUNK
    s_f = sf_scr[...]
    s_b = sb_scr[...]
    for c in range(n):
        rf = slice(c * CHUNK, (c + 1) * CHUNK)
        rb = slice((n - 1 - c) * CHUNK, (n - c) * CHUNK)
        o, s_f = _dn_chunk(qkv_f[rf, 0:256], qkv_f[rf, 256:512], qkv_f[rf, 512:768], bg_f[rf, :], s_f, 0)
        of_ref[rf, :] = o
        o, s_b = _dn_chunk(qkv_b[rb, 0:256], qkv_b[rb, 256:512], qkv_b[rb, 512:768], bg_b[rb, :], s_b, 1)
        ob_ref[rb, :] = o
    sf_scr[...] = s_f
    sb_scr[...] = s_b


def _dn_kernel(cq, cbg, lqf, lbgf, lqb, lbgb, ocf, ocb, olf, olb, sf_scr, sb_scr):
    step = pl.program_id(1)

    @pl.when(step == 0)
    def _():
        sf_scr[...] = jnp.zeros_like(sf_scr)
        sb_scr[...] = jnp.zeros_like(sb_scr)
        _dn_group(cq, cbg, cq, cbg, ocf, ocb, sf_scr, sb_scr)

    @pl.when(step > 0)
    def _():
        _dn_group(lqf, lbgf, lqb, lbgb, olf, olb, sf_scr, sb_scr)


def _scan_specs(width, ngroups):
    ctx = pl.BlockSpec((None, CTX_LEN, width), lambda b, s: (b, 0, 0))
    fwd = pl.BlockSpec((None, GROUP, width), lambda b, s: (b, jnp.maximum(s - 1, 0), 0))
    bwd = pl.BlockSpec((None, GROUP, width), lambda b, s: (b, ngroups - 1 - jnp.maximum(s - 1, 0), 0))
    return ctx, fwd, bwd


def _dn_scan(qkv_c, bg_c, qkv_l, bg_l):
    nb, seq = qkv_l.shape[0], qkv_l.shape[1]
    ng = seq // GROUP
    c768, f768, b768 = _scan_specs(768, ng)
    c128, f128, b128 = _scan_specs(LANES, ng)
    c256, f256, b256 = _scan_specs(256, ng)
    return pl.pallas_call(
        _dn_kernel,
        out_shape=(jax.ShapeDtypeStruct((nb, CTX_LEN, 256), F32), jax.ShapeDtypeStruct((nb, CTX_LEN, 256), F32),
                   jax.ShapeDtypeStruct((nb, seq, 256), F32), jax.ShapeDtypeStruct((nb, seq, 256), F32)),
        grid=(nb, ng + 1),
        in_specs=[c768, c128, f768, f128, b768, b128],
        out_specs=(c256, c256, f256, b256),
        scratch_shapes=[pltpu.VMEM((256, 256), F32), pltpu.VMEM((256, 256), F32)],
        compiler_params=_params(("parallel", "arbitrary")),
        name="dn_scan",
    )(qkv_c, bg_c, qkv_l, bg_l, qkv_l, bg_l)


def _gla_chunk(qk, v, misc, s, d, w2, ba):
    incl, _, _, cum_mat = _tri_consts(d, 256)
    q = qk[:, 0:128] * (GLA_DK ** -0.5)
    k = qk[:, 128:256]
    logit = _dot(misc, w2[:, d * 128:(d + 1) * 128], HI) + ba[:, d * 128:(d + 1) * 128]
    log_a = (jnp.minimum(logit, 0.0) - jnp.log1p(jnp.exp(-jnp.abs(logit)))) * (1.0 / GLA_TAU)
    b = _dot(cum_mat, log_a, HI)
    mid = CHUNK // 2 - 1 if d == 0 else CHUNK // 2
    last = CHUNK - 1 if d == 0 else 0
    b_mid = b[mid:mid + 1]
    b_last = b[last:last + 1]
    ke = _tile4(k * jnp.exp(b_mid - b)) * _group_mask(256, 128, CHUNK, GLA_DK)
    a = jnp.where(incl, _mm_nt(q * jnp.exp(b - b_mid), ke), 0.0)
    o = _mm(a, _tile4(v) * _group_mask(256, 256, CHUNK, CHUNK)) + _mm(q * jnp.exp(b), s)
    ds = _group_mask(128, 256, GLA_DK, CHUNK) * _mm_tn(k * jnp.exp(b_last - b), v)
    eye = (_iota((128, 128), 0) == _iota((128, 128), 1)).astype(F32)
    dec_col = jnp.sum(eye * jnp.exp(b_last), axis=1, keepdims=True)
    return o, dec_col * s + ds


def _gla_group(qk_f, v_f, m_f, qk_b, v_b, m_b, of_ref, ob_ref, sf_scr, sb_scr, w2, ba):
    n = qk_f.shape[0] // CHUNK
    s_f = sf_scr[...]
    s_b = sb_scr[...]
    for c in range(n):
        rf = slice(c * CHUNK, (c + 1) * CHUNK)
        rb = slice((n - 1 - c) * CHUNK, (n - c) * CHUNK)
        o, s_f = _gla_chunk(qk_f[rf, :], v_f[rf, :], m_f[rf, :], s_f, 0, w2, ba)
        of_ref[rf, :] = o
        o, s_b = _gla_chunk(qk_b[rb, :], v_b[rb, :], m_b[rb, :], s_b, 1, w2, ba)
        ob_ref[rb, :] = o
    sf_scr[...] = s_f
    sb_scr[...] = s_b


def _gla_kernel(cqk, cv, cm, lqkf, lvf, lmf, lqkb, lvb, lmb, w2_ref, ba_ref,
                ocf, ocb, olf, olb, sf_scr, sb_scr):
    step = pl.program_id(1)
    w2 = w2_ref[...]
    ba = ba_ref[...]

    @pl.when(step == 0)
    def _():
        sf_scr[...] = jnp.zeros_like(sf_scr)
        sb_scr[...] = jnp.zeros_like(sb_scr)
        _gla_group(cqk, cv, cm, cqk, cv, cm, ocf, ocb, sf_scr, sb_scr, w2, ba)

    @pl.when(step > 0)
    def _():
        _gla_group(lqkf, lvf, lmf, lqkb, lvb, lmb, olf, olb, sf_scr, sb_scr, w2, ba)


def _col_specs(width, col, ngroups):
    blk = col // width
    ctx = pl.BlockSpec((None, CTX_LEN, width), lambda b, s: (b, 0, blk))
    fwd = pl.BlockSpec((None, GROUP, width), lambda b, s: (b, jnp.maximum(s - 1, 0), blk))
    bwd = pl.BlockSpec((None, GROUP, width), lambda b, s: (b, ngroups - 1 - jnp.maximum(s - 1, 0), blk))
    return ctx, fwd, bwd


def _gla_scan(p_c, p_l, w2, ba):
    nb, seq = p_l.shape[0], p_l.shape[1]
    ng = seq // GROUP
    cqk, fqk, bqk = _col_specs(256, C_GQK, ng)
    cv, fv, bv = _col_specs(256, C_GV, ng)
    cm, fm, bm = _col_specs(LANES, C_MISC, ng)
    c256, f256, b256 = _scan_specs(256, ng)
    const = lambda shape: pl.BlockSpec(shape, lambda b, s: (0, 0))
    return pl.pallas_call(
        _gla_kernel,
        out_shape=(jax.ShapeDtypeStruct((nb, CTX_LEN, 256), F32), jax.ShapeDtypeStruct((nb, CTX_LEN, 256), F32),
                   jax.ShapeDtypeStruct((nb, seq, 256), F32), jax.ShapeDtypeStruct((nb, seq, 256), F32)),
        grid=(nb, ng + 1),
        in_specs=[cqk, cv, cm, fqk, fv, fm, bqk, bv, bm, const((LANES, 256)), const((1, 256))],
        out_specs=(c256, c256, f256, b256),
        scratch_shapes=[pltpu.VMEM((128, 256), F32), pltpu.VMEM((128, 256), F32)],
        compiler_params=_params(("parallel", "arbitrary")),
        name="gla_scan",
    )(p_c, p_c, p_c, p_l, p_l, p_l, p_l, p_l, p_l, w2, ba)


def _lane_group(width, group):
    return _iota((1, width), 1) // group


def _softmax_parts(parts):
    m = parts[0].max(axis=-1, keepdims=True)
    for s in parts[1:]:
        m = jnp.maximum(m, s.max(axis=-1, keepdims=True))
    ps = [jnp.exp(s - m) for s in parts]
    total = ps[0].sum(axis=-1, keepdims=True)
    for p in ps[1:]:
        total = total + p.sum(axis=-1, keepdims=True)
    return ps, 1.0 / total


def _na_kernel(q_ref, kl_ref, vl_ref, kc_ref, vc_ref, bias_ref, o_ref, *, grid_rows):
    blk = pl.program_id(1)
    key_row0 = jnp.clip(blk * NA_QROWS - NA_WIN_ROWS // 2, 0, grid_rows - NA_KROWS)
    start = pl.multiple_of(key_row0 * GRID_W, GRID_W)
    kw = kl_ref[pl.ds(start, NA_KROWS * GRID_W), :]
    vw = vl_ref[pl.ds(start, NA_KROWS * GRID_W), :]
    kc = kc_ref[...]
    vc = vc_ref[...]
    q = q_ref[...]
    head = _lane_group(256, NA_DH)
    acc = jnp.zeros(q.shape, F32)
    for h in range(N_HEADS):
        qh = jnp.where(head == h, q, jnp.zeros_like(q))
        (p_w, p_c), inv = _softmax_parts([_dot_nt(qh, kw) + bias_ref[h], _dot_nt(qh, kc)])
        o_h = (_dot(p_w.astype(BF16), vw) + _dot(p_c.astype(BF16), vc)) * inv
        acc = acc + jnp.where(head == h, o_h, 0.0)
    o_ref[...] = acc.astype(BF16)


def _na_attention(qkv_l, qkv_c, bias):
    nb, seq = qkv_l.shape[0], qkv_l.shape[1]
    tq = NA_QROWS * GRID_W
    nblk = seq // tq
    variant = lambda j: jnp.where(j == 0, 0, jnp.where(j == nblk - 1, 2, 1))
    kern = functools.partial(_na_kernel, grid_rows=seq // GRID_W)
    return pl.pallas_call(
        kern,
        out_shape=jax.ShapeDtypeStruct((nb, seq, 256), BF16),
        grid=(nb, nblk),
        in_specs=[pl.BlockSpec((None, tq, 256), lambda b, j: (b, j, 0)),
                  pl.BlockSpec((None, seq, 256), lambda b, j: (b, 0, 1)),
                  pl.BlockSpec((None, seq, 256), lambda b, j: (b, 0, 2)),
                  pl.BlockSpec((None, CTX_LEN, 256), lambda b, j: (b, 0, 1)),
                  pl.BlockSpec((None, CTX_LEN, 256), lambda b, j: (b, 0, 2)),
                  pl.BlockSpec((None, N_HEADS, tq, NA_KROWS * GRID_W), lambda b, j: (variant(j), 0, 0, 0))],
        out_specs=pl.BlockSpec((None, tq, 256), lambda b, j: (b, j, 0)),
        compiler_params=_params(("parallel", "arbitrary")),
        name="na_attention",
    )(qkv_l, qkv_l, qkv_l, qkv_c, qkv_c, bias)


def _na_ctx_kernel(q_ref, k_ref, v_ref, o_ref):
    q = q_ref[...]
    k = k_ref[...]
    v = v_ref[...]
    head = _lane_group(256, NA_DH)
    acc = jnp.zeros(q.shape, F32)
    for h in range(N_HEADS):
        qh = jnp.where(head == h, q, jnp.zeros_like(q))
        (p,), inv = _softmax_parts([_dot_nt(qh, k)])
        acc = acc + jnp.where(head == h, _dot(p.astype(BF16), v) * inv, 0.0)
    o_ref[...] = acc.astype(BF16)


def _na_ctx_attention(qkv_c):
    nb = qkv_c.shape[0]
    spec = lambda c: pl.BlockSpec((None, CTX_LEN, 256), lambda b: (b, 0, c))
    return pl.pallas_call(
        _na_ctx_kernel,
        out_shape=jax.ShapeDtypeStruct((nb, CTX_LEN, 256), BF16),
        grid=(nb,),
        in_specs=[spec(0), spec(1), spec(2)],
        out_specs=spec(0),
        compiler_params=_params(("parallel",)),
        name="na_ctx_attention",
    )(qkv_c, qkv_c, qkv_c)


def _na_bias_tables(rpb, grid_rows):
    a = np.arange(NA_QROWS)[:, None, None, None]
    qc = np.arange(GRID_W)[None, :, None, None]
    b = np.arange(NA_KROWS)[None, None, :, None]
    kc = np.arange(GRID_W)[None, None, None, :]
    shape = (NA_QROWS, GRID_W, NA_KROWS, GRID_W)
    half = NA_WIN_ROWS // 2
    tables = []
    for q_row0 in (0, half, grid_rows - NA_QROWS):
        k_row0 = int(np.clip(q_row0 - half, 0, grid_rows - NA_KROWS))
        qr = q_row0 + a
        kr = k_row0 + b
        rs = np.clip(qr - half, 0, grid_rows - NA_WIN_ROWS)
        cs = np.clip(qc - NA_WIN_COLS // 2, 0, GRID_W - NA_WIN_COLS)
        vis = np.broadcast_to((kr >= rs) & (kr < rs + NA_WIN_ROWS) & (kc >= cs) & (kc < cs + NA_WIN_COLS), shape)
        ri = np.broadcast_to(np.clip(kr - qr + NA_WIN_ROWS - 1, 0, 2 * NA_WIN_ROWS - 2), shape)
        ci = np.broadcast_to(np.clip(kc - qc + NA_WIN_COLS - 1, 0, 2 * NA_WIN_COLS - 2), shape)
        t = jnp.where(vis[None], rpb[:, ri, ci], NEG)
        tables.append(t.reshape(N_HEADS, NA_QROWS * GRID_W, NA_KROWS * GRID_W))
    return jnp.stack(tables)


def _da_kernel(*refs, lam_init, has_lat):
    if has_lat:
        q_ref, kl_ref, vl_ref, kc_ref, vc_ref, lam_ref, g_ref, o_ref = refs
    else:
        q_ref, kc_ref, vc_ref, lam_ref, g_ref, o_ref = refs
    q = q_ref[...]
    tq = q.shape[0]
    lp = lam_ref[...]
    lam = (jnp.exp(jnp.sum(lp[0:1] * lp[1:2], axis=1, keepdims=True))
           - jnp.exp(jnp.sum(lp[2:3] * lp[3:4], axis=1, keepdims=True)) + lam_init)
    qmap = _lane_group(256, DIFF_DQK)
    head = _lane_group(256, 64)
    zero = jnp.zeros_like(q)
    acc = jnp.zeros(q.shape, F32)
    for h in range(N_HEADS):
        qs = jnp.concatenate([jnp.where(qmap == 2 * h, q, zero), jnp.where(qmap == 2 * h + 1, q, zero)], axis=0)
        parts = [_dot_nt(qs, kc_ref[...])]
        if has_lat:
            parts.append(_dot_nt(qs, kl_ref[...]))
        ps, inv = _softmax_parts(parts)
        c0 = inv[0:tq]
        c1 = lam * inv[tq:2 * tq]
        o_h = _dot((ps[0][0:tq] * c0 - ps[0][tq:2 * tq] * c1).astype(BF16), vc_ref[...])
        if has_lat:
            o_h = o_h + _dot((ps[1][0:tq] * c0 - ps[1][tq:2 * tq] * c1).astype(BF16), vl_ref[...])
        acc = acc + jnp.where(head == h, o_h, 0.0)
    y = acc * lax.rsqrt(_group_mean(acc * acc, 64) + EPS) * g_ref[...] * (1.0 - lam_init)
    o_ref[...] = y.astype(BF16)


def _da_attention(qkv_q, qkv_l, qkv_c, lam_params, gain, lam_init, tq):
    nb, nq = qkv_q.shape[0], qkv_q.shape[1]
    has_lat = qkv_l is not None
    full = lambda arr, c: pl.BlockSpec((None, arr.shape[1], 256), lambda b, j: (b, 0, c))
    in_specs = [pl.BlockSpec((None, tq, 256), lambda b, j: (b, j, 0))]
    args = [qkv_q]
    if has_lat:
        in_specs += [full(qkv_l, 1), full(qkv_l, 2)]
        args += [qkv_l, qkv_l]
    in_specs += [full(qkv_c, 1), full(qkv_c, 2),
                 pl.BlockSpec((4, DIFF_DQK), lambda b, j: (0, 0)), pl.BlockSpec((1, 256), lambda b, j: (0, 0))]
    args += [qkv_c, qkv_c, lam_params, gain]
    return pl.pallas_call(
        functools.partial(_da_kernel, lam_init=lam_init, has_lat=has_lat),
        out_shape=jax.ShapeDtypeStruct((nb, nq, 256), BF16),
        grid=(nb, nq // tq),
        in_specs=in_specs,
        out_specs=pl.BlockSpec((None, tq, 256), lambda b, j: (b, j, 0)),
        compiler_params=_params(("parallel", "arbitrary")),
        name="diff_attention",
    )(*args)


def _merge_kernel(x_ref, g1_ref, dnf_ref, dnb_ref, z_ref, glf_ref, glb_ref, r_ref, na_ref, df_ref, gate_ref,
                  bgate_ref, dng_ref, glg_ref, wb_ref, wo_ref, o_ref):
    def head_norm(o, gate, g):
        return ((o * lax.rsqrt(_group_mean(o * o, 64) + EPS) * g) * _silu(gate)).astype(BF16)

    ys = (head_norm(dnf_ref[...] + dnb_ref[...], z_ref[...], dng_ref[...]), na_ref[...],
          head_norm(glf_ref[...] + glb_ref[...], r_ref[...], glg_ref[...]), df_ref[...])
    acc = jnp.zeros(x_ref.shape, F32)
    for g in range(N_BRANCH):
        gate = _sigmoid(gate_ref[:, g * D_MODEL:(g + 1) * D_MODEL] + bgate_ref[g:g + 1, :])
        acc = acc + gate * _dot(ys[g], wb_ref[g])
    o_ref[...] = x_ref[...] + g1_ref[...] * _dot(acc.astype(BF16), wo_ref[...])


def _merge(x2, g1, dn_f, dn_b, gl_f, gl_b, y_na, y_df, p, b_gate, dn_g, gl_g, w_branch, w_out, rows_per_mod, tm):
    rows = x2.shape[0]
    row = lambda w, c=0: pl.BlockSpec((tm, w), lambda i: (i, c))
    const = lambda shape: pl.BlockSpec(shape, lambda i: (0,) * len(shape))
    return pl.pallas_call(
        _merge_kernel,
        out_shape=jax.ShapeDtypeStruct((rows, D_MODEL), F32),
        grid=(rows // tm,),
        in_specs=[row(D_MODEL),
                  pl.BlockSpec((None, 1, D_MODEL), lambda i: ((i * tm) // rows_per_mod, 0, 0)),
                  row(256), row(256), row(256, C_DNZ // 256),
                  row(256), row(256), row(256, C_GR // 256),
                  row(256), row(256), row(N_BRANCH * D_MODEL, C_GATE // (N_BRANCH * D_MODEL)),
                  const((N_BRANCH, D_MODEL)), const((1, 256)), const((1, 256)),
                  const((N_BRANCH, BRANCH_W, D_MODEL)), const((D_MODEL, D_MODEL))],
        out_specs=row(D_MODEL),
        compiler_params=_params(("parallel",)),
        name="merge",
    )(x2, g1, dn_f, dn_b, p, gl_f, gl_b, p, y_na, y_df, p, b_gate, dn_g, gl_g, w_branch, w_out)


FFN_COLS = 256


def _ffn_kernel(x_ref, xp_ref, xn_ref, gain_ref, sc_ref, sh_ref, g2_ref, wu_ref, wv_ref, cw_ref, cb_ref, wo_ref,
                o_ref, u_scr, *, tm, seq_len):
    i = pl.program_id(0)
    first = (i * tm) % seq_len == 0
    last = ((i + 1) * tm) % seq_len == 0

    def norm(x):
        y = x * lax.rsqrt(jnp.mean(x * x, axis=-1, keepdims=True) + EPS)
        return (y * gain_ref[...]) * sc_ref[...] + sh_ref[...]

    x = x_ref[...]
    h = norm(x)
    he = jnp.concatenate([norm(xp_ref[...]), h, norm(xn_ref[...])], axis=0).astype(BF16)
    h = h.astype(BF16)
    rows = _iota((tm + 2 * SUBLANES, 1), 0)
    keep = jnp.logical_not((first & (rows < SUBLANES)) | (last & (rows >= tm + SUBLANES)))
    acc = jnp.zeros(x.shape, F32)
    for c in range(D_FF // FFN_COLS):
        cols = slice(c * FFN_COLS, (c + 1) * FFN_COLS)
        u_scr[...] = jnp.where(keep, _dot(he, wu_ref[:, cols]), 0.0)
        conv = (u_scr[SUBLANES - 1:SUBLANES - 1 + tm, :] * cw_ref[0:1, cols]
                + u_scr[SUBLANES:SUBLANES + tm, :] * cw_ref[1:2, cols]
                + u_scr[SUBLANES + 1:SUBLANES + 1 + tm, :] * cw_ref[2:3, cols]) + cb_ref[:, cols]
        act = (_silu(conv) * _dot(h, wv_ref[:, cols])).astype(BF16)
        acc = acc + _dot(act, wo_ref[cols, :])
    o_ref[...] = x + g2_ref[...] * acc


def _ffn(x2, gain, scale1p, shift, g2, w_u, w_v, conv_w, conv_b, w_o, rows_per_mod, seq_len, tm):
    rows = x2.shape[0]
    nb8 = rows // SUBLANES
    t8 = tm // SUBLANES
    const = lambda shape: pl.BlockSpec(shape, lambda i: (0,) * len(shape))
    mod = pl.BlockSpec((None, 1, D_MODEL), lambda i: ((i * tm) // rows_per_mod, 0, 0))
    return pl.pallas_call(
        functools.partial(_ffn_kernel, tm=tm, seq_len=seq_len),
        out_shape=jax.ShapeDtypeStruct((rows, D_MODEL), F32),
        grid=(rows // tm,),
        in_specs=[pl.BlockSpec((tm, D_MODEL), lambda i: (i, 0)),
                  pl.BlockSpec((SUBLANES, D_MODEL), lambda i: (jnp.maximum(i * t8 - 1, 0), 0)),
                  pl.BlockSpec((SUBLANES, D_MODEL), lambda i: (jnp.minimum((i + 1) * t8, nb8 - 1), 0)),
                  const((1, D_MODEL)), mod, mod, mod,
                  const((D_MODEL, D_FF)), const((D_MODEL, D_FF)), const((3, D_FF)), const((1, D_FF)),
                  const((D_FF, D_MODEL))],
        out_specs=pl.BlockSpec((tm, D_MODEL), lambda i: (i, 0)),
        scratch_shapes=[pltpu.VMEM((tm + 2 * SUBLANES, FFN_COLS), F32)],
        compiler_params=_params(("parallel",)),
        name="conv_ffn",
    )(x2, x2, x2, gain, scale1p, shift, g2, w_u, w_v, conv_w, conv_b, w_o)


def _inproj_columns():
    widths = (768, 256, 8, 8, 768, 128, 128, 256, 256, 32, 768, 4096)
    o = np.concatenate([[0], np.cumsum(widths)])
    src = np.zeros(P_WIDTH, np.int64)
    sign = np.zeros(P_WIDTH, np.float32)

    def put(dst, cols, sgn=None):
        src[dst:dst + len(cols)] = cols
        sign[dst:dst + len(cols)] = 1.0 if sgn is None else sgn

    put(C_DN, np.arange(o[0], o[1]))
    put(C_NA, np.arange(o[4], o[5]))
    put(C_DF, np.arange(o[10], o[11]))
    put(C_DNZ, np.arange(o[1], o[2]))
    quarter = DIFF_DQK // 4
    perm = np.concatenate([np.arange(quarter, 2 * quarter), np.arange(0, quarter),
                           np.arange(3 * quarter, 4 * quarter), np.arange(2 * quarter, 3 * quarter)])
    sgn = np.concatenate([-np.ones(quarter), np.ones(quarter), -np.ones(quarter), np.ones(quarter)])
    j = np.arange(512)
    put(C_DFROT, o[10] + (j // DIFF_DQK) * DIFF_DQK + perm[j % DIFF_DQK], sgn[j % DIFF_DQK])
    put(C_GQK, np.arange(o[5], o[7]))
    put(C_GV, np.arange(o[7], o[8]))
    put(C_GR, np.arange(o[8], o[9]))
    put(C_MISC, np.concatenate([np.arange(o[2], o[4]), np.arange(o[9], o[10])]))
    put(C_GATE, np.arange(o[11], o[12]))
    return src, sign, perm


def _rope_tables(length):
    t = jnp.arange(length)
    row = (t // GRID_W).astype(F32)
    col = (t % GRID_W).astype(F32)
    n_freq = DIFF_DQK // 4
    inv_freq = jnp.power(jnp.float32(ROPE_THETA), -jnp.arange(n_freq, dtype=F32) / n_freq)
    ang_r = row[:, None] * inv_freq
    ang_c = col[:, None] * inv_freq
    ang = jnp.concatenate([ang_r, ang_r, ang_c, ang_c], axis=-1)
    reps = 256 // DIFF_DQK
    return jnp.tile(jnp.cos(ang), (1, reps)), jnp.tile(jnp.sin(ang), (1, reps))


def _pad_lanes(v, offset, width):
    return jnp.zeros((1, width), F32).at[0, offset:offset + v.shape[0]].set(v)


def kernel(x, c, ctx, c_ctx, w_mod, b_mod, norm1_g, norm2_g, w_in, b_gate, dn_conv, dn_a_log, dn_dt_bias, dn_norm_g,
           na_q_norm, na_k_norm, na_rpb, gla_w_a2, gla_b_a, gla_norm_g, df_q_norm, df_k_norm, df_lambda, df_norm_g,
           w_branch, w_out, ffn_w_in, ffn_conv_w, ffn_conv_b, ffn_w_out):
    nb, seq, _ = x.shape
    n_lat, n_ctx = nb * seq, nb * CTX_LEN
    tm_lat, tm_ctx = 512, 256

    cc = jnp.zeros((16, D_MODEL), F32).at[0:nb].set(c).at[nb].set(c_ctx)
    mod = _modulation(cc, w_mod, b_mod)

    src, sign, perm = _inproj_columns()
    w_in_p = (jnp.take(w_in, jnp.asarray(src), axis=2) * jnp.asarray(sign)).astype(BF16)
    w_branch_b = w_branch.astype(BF16)
    w_out_b = w_out.astype(BF16)
    w_u = ffn_w_in[:, :, 0:D_FF].astype(BF16)
    w_v = ffn_w_in[:, :, D_FF:2 * D_FF].astype(BF16)
    w_o = ffn_w_out.astype(BF16)
    cos, sin = _rope_tables(seq)

    x2 = x.reshape(n_lat, D_MODEL)
    xc2 = ctx.reshape(n_ctx, D_MODEL)
    for li in range(DEPTH):
        with_ctx = li < DEPTH - 1
        lam_init = 0.8 - 0.6 * math.exp(-0.3 * li)
        m = mod[li]
        part = lambda k: m[:, k * D_MODEL:(k + 1) * D_MODEL]
        lat = lambda v: v[0:nb].reshape(nb, 1, D_MODEL)
        cx = lambda v: v[nb:nb + 1].reshape(1, 1, D_MODEL)
        sh1, sc1, g1, sh2, sc2, g2 = (part(k) for k in range(6))
        n1 = norm1_g[li].reshape(1, D_MODEL)
        n2 = norm2_g[li].reshape(1, D_MODEL)

        p_l = _inproj(x2, n1, lat(1.0 + sc1), lat(sh1), w_in_p[li], seq, tm_lat)
        p_c = _inproj(xc2, n1, cx(1.0 + sc1), cx(sh1), w_in_p[li], n_ctx, tm_ctx)

        tile4 = lambda v: jnp.tile(v, 256 // v.shape[0]).reshape(1, 256)
        vecs = (dn_conv[li], _pad_lanes(dn_a_log[li].reshape(-1), 2 * N_HEADS, LANES),
                _pad_lanes(dn_dt_bias[li].reshape(-1), 2 * N_HEADS, LANES),
                tile4(na_q_norm[li]), tile4(na_k_norm[li]), tile4(df_q_norm[li]), tile4(df_k_norm[li]),
                tile4(df_q_norm[li][perm]), tile4(df_k_norm[li][perm]))
        dn_l, bg_l, na_l, df_l = _prep(p_l, cos, sin, vecs, tp=tm_lat, seq_len=seq, rope=True)
        dn_c, bg_c, na_c, df_c = _prep(p_c, cos, sin, vecs, tp=tm_ctx, seq_len=CTX_LEN, rope=False)

        b3 = lambda a, t: a.reshape(nb, t, a.shape[-1])
        dn_cf, dn_cb, dn_lf, dn_lb = _dn_scan(b3(dn_c, CTX_LEN), b3(bg_c, CTX_LEN), b3(dn_l, seq), b3(bg_l, seq))

        w2 = jnp.zeros((LANES, 256), F32)
        a1_off = 4 * N_HEADS
        w2 = w2.at[a1_off:a1_off + GLA_RANK, 0:128].set(gla_w_a2[li, 0])
        w2 = w2.at[a1_off + GLA_RANK:a1_off + 2 * GLA_RANK, 128:256].set(gla_w_a2[li, 1])
        gl_cf, gl_cb, gl_lf, gl_lb = _gla_scan(b3(p_c, CTX_LEN), b3(p_l, seq), w2, gla_b_a[li].reshape(1, 256))

        y_na = _na_attention(b3(na_l, seq), b3(na_c, CTX_LEN), _na_bias_tables(na_rpb[li], seq // GRID_W))
        y_df = _da_attention(b3(df_l, seq), b3(df_l, seq), b3(df_c, CTX_LEN), df_lambda[li],
                             tile4(df_norm_g[li]), lam_init, 128)

        flat = lambda a: a.reshape(-1, a.shape[-1])
        gains = (b_gate[li], tile4(dn_norm_g[li]), tile4(gla_norm_g[li]), w_branch_b[li], w_out_b[li])
        ffn_w = (w_u[li], w_v[li], ffn_conv_w[li], ffn_conv_b[li].reshape(1, D_FF), w_o[li])
        x2 = _merge(x2, lat(g1), flat(dn_lf), flat(dn_lb), flat(gl_lf), flat(gl_lb), flat(y_na), flat(y_df), p_l,
                    *gains, seq, tm_lat)
        x2 = _ffn(x2, n2, lat(1.0 + sc2), lat(sh2), lat(g2), *ffn_w, seq, seq, tm_lat)
        if with_ctx:
            yc_na = _na_ctx_attention(b3(na_c, CTX_LEN))
            yc_df = _da_attention(b3(df_c, CTX_LEN), None, b3(df_c, CTX_LEN), df_lambda[li],
                                  tile4(df_norm_g[li]), lam_init, 128)
            xc2 = _merge(xc2, cx(g1), flat(dn_cf), flat(dn_cb), flat(gl_cf), flat(gl_cb), flat(yc_na), flat(yc_df),
                         p_c, *gains, n_ctx, tm_ctx)
            xc2 = _ffn(xc2, n2, cx(1.0 + sc2), cx(sh2), cx(g2), *ffn_w, n_ctx, CTX_LEN, tm_ctx)
    return x2.reshape(nb, seq, D_MODEL)
```

```python
import functools
import math

import numpy as np
import jax
import jax.numpy as jnp
from jax import lax
from jax.experimental import pallas as pl
from jax.experimental.pallas import tpu as pltpu

F32 = jnp.float32
BF16 = jnp.bfloat16
HI = lax.Precision.HIGHEST

D_MODEL = 1024
DEPTH = 4
GRID_W = 64
CTX_LEN = 256
N_BRANCH = 4
BRANCH_W = 256
CHUNK = 64
N_HEADS = 4
DN_DK = 64
DN_CONV = 3
NA_DH = 64
NA_WIN_ROWS = 8
NA_WIN_COLS = 16
GLA_DK = 32
GLA_RANK = 16
GLA_TAU = 16.0
DIFF_DQK = 32
ROPE_THETA = 10000.0
D_FF = 2816
EPS = 1e-6
NEG = -1e30
LOG2E = 1.4426950408889634

LANES = 128
SUBLANES = 8
VMEM_LIMIT = 56 * 1024 * 1024

C_DN = 0
C_NA = 768
C_DF = 1536
C_DNZ = 2304
C_DFROT = 2560
C_GQK = 3072
C_GV = 3328
C_GR = 3584
C_MISC = 3840
C_GATE = 4096
P_WIDTH = 8192

GROUP = 4 * CHUNK
NA_QROWS = 4
NA_KROWS = 12


def _dot(a, b, prec=None):
    return jnp.dot(a, b, preferred_element_type=F32, precision=prec)


def _dot_nt(a, b, prec=None):
    return lax.dot_general(a, b, (((1,), (1,)), ((), ())), preferred_element_type=F32, precision=prec)


def _dot_tn(a, b, prec=None):
    return lax.dot_general(a, b, (((0,), (0,)), ((), ())), preferred_element_type=F32, precision=prec)


def _iota(shape, dim):
    return lax.broadcasted_iota(jnp.int32, shape, dim)


def _group_mask(rows, cols, rgroup, cgroup):
    return (_iota((rows, cols), 0) // rgroup == _iota((rows, cols), 1) // cgroup).astype(F32)


def _group_mean(xsq, width):
    n = xsq.shape[-1]
    ones = _group_mask(n, n, width, width)
    return _dot(xsq, ones, HI) * (1.0 / width)


def _sigmoid(x):
    return 1.0 / (1.0 + jnp.exp(-x))


def _silu(x):
    return x * _sigmoid(x)


def _softplus(x):
    return jnp.maximum(x, 0.0) + jnp.log1p(jnp.exp(-jnp.abs(x)))


def _params(sem):
    return pltpu.CompilerParams(dimension_semantics=sem, vmem_limit_bytes=VMEM_LIMIT)


def _mod_kernel(c_ref, w_ref, b_ref, o_ref):
    o_ref[...] = _dot(_silu(c_ref[...]), w_ref[...], HI) + b_ref[...]


def _modulation(cc, w_mod, b_mod):
    tn = 1024
    return pl.pallas_call(
        _mod_kernel,
        out_shape=jax.ShapeDtypeStruct((DEPTH, 16, 6 * D_MODEL), F32),
        grid=(DEPTH, 6 * D_MODEL // tn),
        in_specs=[pl.BlockSpec((16, D_MODEL), lambda l, j: (0, 0)),
                  pl.BlockSpec((None, D_MODEL, tn), lambda l, j: (l, 0, j)),
                  pl.BlockSpec((None, 1, tn), lambda l, j: (l, 0, j))],
        out_specs=pl.BlockSpec((None, 16, tn), lambda l, j: (l, 0, j)),
        compiler_params=_params(("arbitrary", "arbitrary")),
        name="modulation",
    )(cc, w_mod, b_mod.reshape(DEPTH, 1, 6 * D_MODEL))


def _inproj_kernel(x_ref, g_ref, sc_ref, sh_ref, w_ref, o_ref, h_scr):
    @pl.when(pl.program_id(1) == 0)
    def _():
        x = x_ref[...]
        y = x * lax.rsqrt(jnp.mean(x * x, axis=-1, keepdims=True) + EPS)
        h_scr[...] = ((y * g_ref[...]) * sc_ref[...] + sh_ref[...]).astype(BF16)

    o_ref[...] = _dot(h_scr[...], w_ref[...])


def _inproj(x2, gain, scale1p, shift, w, rows_per_mod, tm):
    rows = x2.shape[0]
    tn = 1024
    mod_spec = pl.BlockSpec((None, 1, D_MODEL), lambda i, j: ((i * tm) // rows_per_mod, 0, 0))
    return pl.pallas_call(
        _inproj_kernel,
        out_shape=jax.ShapeDtypeStruct((rows, P_WIDTH), F32),
        grid=(rows // tm, P_WIDTH // tn),
        in_specs=[pl.BlockSpec((tm, D_MODEL), lambda i, j: (i, 0)),
                  pl.BlockSpec((1, D_MODEL), lambda i, j: (0, 0)),
                  mod_spec, mod_spec,
                  pl.BlockSpec((D_MODEL, tn), lambda i, j: (0, j))],
        out_specs=pl.BlockSpec((tm, tn), lambda i, j: (i, j)),
        scratch_shapes=[pltpu.VMEM((tm, D_MODEL), BF16)],
        compiler_params=_params(("parallel", "arbitrary")),
        name="inproj",
    )(x2, gain, scale1p, shift, w)


def _prep_kernel(dn_ref, dnp_ref, dnn_ref, misc_ref, na_ref, df_ref, dfrot_ref, cos_ref, sin_ref,
                 convw_ref, alog_ref, dt_ref, naq_ref, nak_ref, dfq_ref, dfk_ref, dfqr_ref, dfkr_ref,
                 dn_o, bg_o, na_o, df_o, xe_scr, *, tp, seq_len, rope):
    i = pl.program_id(0)
    first = (i * tp) % seq_len == 0
    last = ((i + 1) * tp) % seq_len == 0
    xe_scr[0:SUBLANES, :] = jnp.where(first, 0.0, dnp_ref[...])
    xe_scr[SUBLANES:SUBLANES + tp, :] = dn_ref[...]
    xe_scr[SUBLANES + tp:2 * SUBLANES + tp, :] = jnp.where(last, 0.0, dnn_ref[...])
    cw = convw_ref[...]
    conv = (xe_scr[SUBLANES - 1:SUBLANES - 1 + tp, :] * cw[0:1]
            + xe_scr[SUBLANES:SUBLANES + tp, :] * cw[1:2]
            + xe_scr[SUBLANES + 1:SUBLANES + 1 + tp, :] * cw[2:3])
    s = _silu(conv)
    q, k, v = s[:, 0:256], s[:, 256:512], s[:, 512:768]
    qn = q * lax.rsqrt(_group_mean(q * q, DN_DK) * DN_DK + EPS) * (DN_DK ** -0.5)
    kn = k * lax.rsqrt(_group_mean(k * k, DN_DK) * DN_DK + EPS)
    dn_o[:, 0:256] = qn
    dn_o[:, 256:512] = kn
    dn_o[:, 512:768] = v
    m = misc_ref[...]
    beta = _sigmoid(m)
    g = -jnp.exp(alog_ref[...]) * _softplus(m + dt_ref[...])
    bg_o[...] = jnp.where(_iota(m.shape, 1) < 2 * N_HEADS, beta, g)
    x = na_ref[...]
    q, k, v = x[:, 0:256], x[:, 256:512], x[:, 512:768]
    qn = q * lax.rsqrt(_group_mean(q * q, NA_DH) + EPS) * naq_ref[...] * (NA_DH ** -0.5)
    kn = k * lax.rsqrt(_group_mean(k * k, NA_DH) + EPS) * nak_ref[...]
    na_o[:, 0:256] = qn.astype(BF16)
    na_o[:, 256:512] = kn.astype(BF16)
    na_o[:, 512:768] = v.astype(BF16)
    x = df_ref[...]
    q, k, v = x[:, 0:256], x[:, 256:512], x[:, 512:768]
    rq = lax.rsqrt(_group_mean(q * q, DIFF_DQK) + EPS)
    rk = lax.rsqrt(_group_mean(k * k, DIFF_DQK) + EPS)
    qn = q * rq * dfq_ref[...]
    kn = k * rk * dfk_ref[...]
    if rope:
        xr = dfrot_ref[...]
        cos, sin = cos_ref[...], sin_ref[...]
        qn = qn * cos + (xr[:, 0:256] * rq * dfqr_ref[...]) * sin
        kn = kn * cos + (xr[:, 256:512] * rk * dfkr_ref[...]) * sin
    df_o[:, 0:256] = (qn * (DIFF_DQK ** -0.5 * LOG2E)).astype(BF16)
    df_o[:, 256:512] = kn.astype(BF16)
    df_o[:, 512:768] = v.astype(BF16)


def _prep(p, cos, sin, vecs, *, tp, seq_len, rope):
    rows = p.shape[0]
    nb8 = rows // SUBLANES
    t8 = tp // SUBLANES
    nseq_tiles = max(seq_len // tp, 1)
    row_vec = lambda w: pl.BlockSpec((1, w), lambda i: (0, 0))
    kern = functools.partial(_prep_kernel, tp=tp, seq_len=seq_len, rope=rope)
    return pl.pallas_call(
        kern,
        out_shape=(jax.ShapeDtypeStruct((rows, 768), F32), jax.ShapeDtypeStruct((rows, LANES), F32),
                   jax.ShapeDtypeStruct((rows, 768), BF16), jax.ShapeDtypeStruct((rows, 768), BF16)),
        grid=(rows // tp,),
        in_specs=[pl.BlockSpec((tp, 768), lambda i: (i, C_DN // 768)),
                  pl.BlockSpec((SUBLANES, 768), lambda i: (jnp.maximum(i * t8 - 1, 0), C_DN // 768)),
                  pl.BlockSpec((SUBLANES, 768), lambda i: (jnp.minimum((i + 1) * t8, nb8 - 1), C_DN // 768)),
                  pl.BlockSpec((tp, LANES), lambda i: (i, C_MISC // LANES)),
                  pl.BlockSpec((tp, 768), lambda i: (i, C_NA // 768)),
                  pl.BlockSpec((tp, 768), lambda i: (i, C_DF // 768)),
                  pl.BlockSpec((tp, 512), lambda i: (i, C_DFROT // 512)),
                  pl.BlockSpec((tp, 256), lambda i: (i % nseq_tiles, 0)),
                  pl.BlockSpec((tp, 256), lambda i: (i % nseq_tiles, 0)),
                  pl.BlockSpec((DN_CONV, 768), lambda i: (0, 0)),
                  row_vec(LANES), row_vec(LANES),
                  row_vec(256), row_vec(256), row_vec(256), row_vec(256), row_vec(256), row_vec(256)],
        out_specs=(pl.BlockSpec((tp, 768), lambda i: (i, 0)), pl.BlockSpec((tp, LANES), lambda i: (i, 0)),
                   pl.BlockSpec((tp, 768), lambda i: (i, 0)), pl.BlockSpec((tp, 768), lambda i: (i, 0))),
        scratch_shapes=[pltpu.VMEM((tp + 2 * SUBLANES, 768), F32)],
        compiler_params=_params(("parallel",)),
        name="prep",
    )(p, p, p, p, p, p, p, cos, sin, *vecs)


def _mm(a, b):
    return _dot(a.astype(BF16), b.astype(BF16))


def _mm_nt(a, b):
    return _dot_nt(a.astype(BF16), b.astype(BF16))


def _mm_tn(a, b):
    return _dot_tn(a.astype(BF16), b.astype(BF16))


def _tile4(x):
    return jnp.concatenate([x, x, x, x], axis=0)


def _tri_consts(d, width):
    rows = _iota((CHUNK, width), 0)
    lane_j = _iota((CHUNK, width), 1) % CHUNK
    incl = (lane_j <= rows) if d == 0 else (lane_j >= rows)
    strict = (lane_j < rows) if d == 0 else (lane_j > rows)
    eye_cat = (lane_j == rows).astype(F32)
    r64 = _iota((CHUNK, CHUNK), 0)
    c64 = _iota((CHUNK, CHUNK), 1)
    cum_mat = ((c64 <= r64) if d == 0 else (c64 >= r64)).astype(F32)
    return incl, strict, eye_cat, cum_mat


def _dn_group(qkv_f, bg_f, qkv_b, bg_b, of_ref, ob_ref, sf_scr, sb_scr):
    t = qkv_f.shape[0]
    n = t // CHUNK
    rows = _iota((CHUNK, 256), 0)
    lane_j = _iota((CHUNK, 256), 1) % CHUNK
    eye_cat = (lane_j == rows).astype(F32)
    mask_bd = _iota((256, 256), 0) // CHUNK == _iota((256, 256), 1) // CHUNK
    zero_b = jnp.zeros((), BF16)

    def bd(x):
        return jnp.where(mask_bd, _tile4(x.astype(BF16)), zero_b)

    work = []
    for d, (qkv, bg) in enumerate(((qkv_f, bg_f), (qkv_b, bg_b))):
        incl = (lane_j <= rows) if d == 0 else (lane_j >= rows)
        strict = (lane_j < rows) if d == 0 else (lane_j > rows)
        srow = _iota((LANES, 512), 0)
        scol = _iota((LANES, 512), 1)
        sel = (srow == (scol // 256) * 2 * N_HEADS + d * N_HEADS + (scol % 256) // CHUNK).astype(F32)
        e = _dot(bg[...], sel, HI)
        rt = _iota((t, t), 0)
        ct = _iota((t, t), 1)
        cum = ((rt // CHUNK == ct // CHUNK) & ((ct <= rt) if d == 0 else (ct >= rt))).astype(F32)
        gc_all = _dot(cum, e[:, 256:512], HI)
        for c in (range(n) if d == 0 else range(n - 1, -1, -1)):
            sl = slice(c * CHUNK, (c + 1) * CHUNK)
            q, k, v = qkv[sl, 0:256], qkv[sl, 256:512], qkv[sl, 512:768]
            beta, gc = e[sl, 0:256], gc_all[sl]
            gc_row = jnp.sum(eye_cat * gc, axis=0, keepdims=True)
            decay = jnp.exp(jnp.where(incl, gc - gc_row, -jnp.inf))
            gc_last = gc[CHUNK - 1:CHUNK] if d == 0 else gc[0:1]
            kb = k * beta
            gram = _dot_nt(jnp.concatenate([kb, q], axis=0).astype(BF16), bd(k))
            n_cat = jnp.where(strict, gram[0:CHUNK] * decay, 0.0)
            egc = jnp.exp(gc)
            work.append(dict(
                d=d, sl=sl, n_cat=n_cat, a_cat=jnp.where(incl, gram[CHUNK:2 * CHUNK] * decay, 0.0).astype(BF16),
                rhs_u=bd(v * beta), rhs_w=bd(kb * egc), qd=(q * egc).astype(BF16),
                kd=(k * jnp.exp(gc_last - gc)).astype(BF16), s_decay=jnp.exp(gc_last),
                p=eye_cat - jnp.where(rows // 2 == lane_j // 2, n_cat, 0.0)))
    size = 2
    while size < CHUNK:
        off = (rows // (2 * size) == lane_j // (2 * size)) & (rows // size != lane_j // size)
        half = [_dot(w["p"].astype(BF16), bd(jnp.where(off, w["n_cat"], 0.0))) for w in work]
        for w, hx in zip(work, half):
            w["p"] = w["p"] - _dot(hx.astype(BF16), bd(w["p"]))
        size *= 2
    for w in work:
        pb = w["p"].astype(BF16)
        w["u"] = _dot(pb, w["rhs_u"])
        w["w"] = _dot(pb, w["rhs_w"]).astype(BF16)
    state = [sf_scr[...], sb_scr[...]]
    out = (of_ref, ob_ref)
    for c in range(n):
        for d in range(2):
            w = work[d * n + c]
            sb = state[d].astype(BF16)
            v_new = w["u"] - _dot(w["w"], sb)
            out[d][w["sl"], :] = _dot(w["qd"], sb) + _dot(w["a_cat"], bd(v_new))
            upd = _dot_tn(w["kd"], v_new.astype(BF16))
            state[d] = state[d] * w["s_decay"] + jnp.where(mask_bd, upd, 0.0)
    sf_scr[...] = state[0]
    sb_scr[...] = state[1]


def _dn_kernel(cq, cbg, lqf, lbgf, lqb, lbgb, ocf, ocb, olf, olb, sf_scr, sb_scr):
    step = pl.program_id(1)

    @pl.when(step == 0)
    def _():
        sf_scr[...] = jnp.zeros_like(sf_scr)
        sb_scr[...] = jnp.zeros_like(sb_scr)
        _dn_group(cq, cbg, cq, cbg, ocf, ocb, sf_scr, sb_scr)

    @pl.when(step > 0)
    def _():
        _dn_group(lqf, lbgf, lqb, lbgb, olf, olb, sf_scr, sb_scr)


def _scan_specs(width, ngroups):
    ctx = pl.BlockSpec((None, CTX_LEN, width), lambda b, s: (b, 0, 0))
    fwd = pl.BlockSpec((None, GROUP, width), lambda b, s: (b, jnp.maximum(s - 1, 0), 0))
    bwd = pl.BlockSpec((None, GROUP, width), lambda b, s: (b, ngroups - 1 - jnp.maximum(s - 1, 0), 0))
    return ctx, fwd, bwd


def _dn_scan(qkv_c, bg_c, qkv_l, bg_l):
    nb, seq = qkv_l.shape[0], qkv_l.shape[1]
    ng = seq // GROUP
    c768, f768, b768 = _scan_specs(768, ng)
    c128, f128, b128 = _scan_specs(LANES, ng)
    c256, f256, b256 = _scan_specs(256, ng)
    return pl.pallas_call(
        _dn_kernel,
        out_shape=(jax.ShapeDtypeStruct((nb, CTX_LEN, 256), F32), jax.ShapeDtypeStruct((nb, CTX_LEN, 256), F32),
                   jax.ShapeDtypeStruct((nb, seq, 256), F32), jax.ShapeDtypeStruct((nb, seq, 256), F32)),
        grid=(nb, ng + 1),
        in_specs=[c768, c128, f768, f128, b768, b128],
        out_specs=(c256, c256, f256, b256),
        scratch_shapes=[pltpu.VMEM((256, 256), F32), pltpu.VMEM((256, 256), F32)],
        compiler_params=_params(("parallel", "arbitrary")),
        name="dn_scan",
    )(qkv_c, bg_c, qkv_l, bg_l, qkv_l, bg_l)


def _gla_group(qk_f, v_f, m_f, qk_b, v_b, m_b, of_ref, ob_ref, sf_scr, sb_scr, w2, ba):
    t = qk_f.shape[0]
    n = t // CHUNK
    rows = _iota((CHUNK, 256), 0)
    lane_j = _iota((CHUNK, 256), 1) % CHUNK
    mask_k = _iota((256, 128), 0) // CHUNK == _iota((256, 128), 1) // GLA_DK
    mask_v = _iota((256, 256), 0) // CHUNK == _iota((256, 256), 1) // CHUNK
    mask_s = _iota((128, 256), 0) // GLA_DK == _iota((128, 256), 1) // CHUNK
    eye = (_iota((128, 128), 0) == _iota((128, 128), 1)).astype(F32)
    zero_b = jnp.zeros((), BF16)
    work = []
    for d, (qk, v, misc) in enumerate(((qk_f, v_f, m_f), (qk_b, v_b, m_b))):
        incl = (lane_j <= rows) if d == 0 else (lane_j >= rows)
        logit = _dot(misc[...], w2[:, d * 128:(d + 1) * 128], HI) + ba[:, d * 128:(d + 1) * 128]
        log_a = (jnp.minimum(logit, 0.0) - jnp.log1p(jnp.exp(-jnp.abs(logit)))) * (1.0 / GLA_TAU)
        rt = _iota((t, t), 0)
        ct = _iota((t, t), 1)
        cum = ((rt // CHUNK == ct // CHUNK) & ((ct <= rt) if d == 0 else (ct >= rt))).astype(F32)
        b_all = _dot(cum, log_a, HI)
        mid = CHUNK // 2 - 1 if d == 0 else CHUNK // 2
        last = CHUNK - 1 if d == 0 else 0
        for c in (range(n) if d == 0 else range(n - 1, -1, -1)):
            sl = slice(c * CHUNK, (c + 1) * CHUNK)
            q = qk[sl, 0:128] * (GLA_DK ** -0.5)
            k = qk[sl, 128:256]
            vb = v[sl, :].astype(BF16)
            b = b_all[sl]
            b_mid = b[mid:mid + 1]
            b_last = b[last:last + 1]
            ke = jnp.where(mask_k, _tile4((k * jnp.exp(b_mid - b)).astype(BF16)), zero_b)
            a = jnp.where(incl, _dot_nt((q * jnp.exp(b - b_mid)).astype(BF16), ke), 0.0)
            o_intra = _dot(a.astype(BF16), jnp.where(mask_v, _tile4(vb), zero_b))
            ds = jnp.where(mask_s, _dot_tn((k * jnp.exp(b_last - b)).astype(BF16), vb), 0.0)
            dec_col = jnp.sum(eye * jnp.exp(b_last), axis=1, keepdims=True)
            work.append(dict(sl=sl, o_intra=o_intra, qb=(q * jnp.exp(b)).astype(BF16), ds=ds, dec_col=dec_col))
    state = [sf_scr[...], sb_scr[...]]
    out = (of_ref, ob_ref)
    for c in range(n):
        for d in range(2):
            w = work[d * n + c]
            out[d][w["sl"], :] = w["o_intra"] + _dot(w["qb"], state[d].astype(BF16))
            state[d] = w["dec_col"] * state[d] + w["ds"]
    sf_scr[...] = state[0]
    sb_scr[...] = state[1]


def _gla_kernel(cqk, cv, cm, lqkf, lvf, lmf, lqkb, lvb, lmb, w2_ref, ba_ref,
                ocf, ocb, olf, olb, sf_scr, sb_scr):
    step = pl.program_id(1)
    w2 = w2_ref[...]
    ba = ba_ref[...]

    @pl.when(step == 0)
    def _():
        sf_scr[...] = jnp.zeros_like(sf_scr)
        sb_scr[...] = jnp.zeros_like(sb_scr)
        _gla_group(cqk, cv, cm, cqk, cv, cm, ocf, ocb, sf_scr, sb_scr, w2, ba)

    @pl.when(step > 0)
    def _():
        _gla_group(lqkf, lvf, lmf, lqkb, lvb, lmb, olf, olb, sf_scr, sb_scr, w2, ba)


def _col_specs(width, col, ngroups):
    blk = col // width
    ctx = pl.BlockSpec((None, CTX_LEN, width), lambda b, s: (b, 0, blk))
    fwd = pl.BlockSpec((None, GROUP, width), lambda b, s: (b, jnp.maximum(s - 1, 0), blk))
    bwd = pl.BlockSpec((None, GROUP, width), lambda b, s: (b, ngroups - 1 - jnp.maximum(s - 1, 0), blk))
    return ctx, fwd, bwd


def _gla_scan(p_c, p_l, w2, ba):
    nb, seq = p_l.shape[0], p_l.shape[1]
    ng = seq // GROUP
    cqk, fqk, bqk = _col_specs(256, C_GQK, ng)
    cv, fv, bv = _col_specs(256, C_GV, ng)
    cm, fm, bm = _col_specs(LANES, C_MISC, ng)
    c256, f256, b256 = _scan_specs(256, ng)
    const = lambda shape: pl.BlockSpec(shape, lambda b, s: (0, 0))
    return pl.pallas_call(
        _gla_kernel,
        out_shape=(jax.ShapeDtypeStruct((nb, CTX_LEN, 256), F32), jax.ShapeDtypeStruct((nb, CTX_LEN, 256), F32),
                   jax.ShapeDtypeStruct((nb, seq, 256), F32), jax.ShapeDtypeStruct((nb, seq, 256), F32)),
        grid=(nb, ng + 1),
        in_specs=[cqk, cv, cm, fqk, fv, fm, bqk, bv, bm, const((LANES, 256)), const((1, 256))],
        out_specs=(c256, c256, f256, b256),
        scratch_shapes=[pltpu.VMEM((128, 256), F32), pltpu.VMEM((128, 256), F32)],
        compiler_params=_params(("parallel", "arbitrary")),
        name="gla_scan",
    )(p_c, p_c, p_c, p_l, p_l, p_l, p_l, p_l, p_l, w2, ba)


def _lane_group(width, group):
    return _iota((1, width), 1) // group


def _softmax_parts(parts):
    m = parts[0].max(axis=-1, keepdims=True)
    for s in parts[1:]:
        m = jnp.maximum(m, s.max(axis=-1, keepdims=True))
    ps = [jnp.exp(s - m) for s in parts]
    total = ps[0].sum(axis=-1, keepdims=True)
    for p in ps[1:]:
        total = total + p.sum(axis=-1, keepdims=True)
    return ps, 1.0 / total


def _na_kernel(q_ref, kl_ref, vl_ref, kc_ref, vc_ref, bias_ref, o_ref, *, grid_rows):
    blk = pl.program_id(1)
    key_row0 = jnp.clip(blk * NA_QROWS - NA_WIN_ROWS // 2, 0, grid_rows - NA_KROWS)
    start = pl.multiple_of(key_row0 * GRID_W, GRID_W)
    kw = kl_ref[pl.ds(start, NA_KROWS * GRID_W), :]
    vw = vl_ref[pl.ds(start, NA_KROWS * GRID_W), :]
    kc = kc_ref[...]
    vc = vc_ref[...]
    q = q_ref[...]
    head = _lane_group(256, NA_DH)
    acc = jnp.zeros(q.shape, F32)
    for h in range(N_HEADS):
        qh = jnp.where(head == h, q, jnp.zeros_like(q))
        (p_w, p_c), inv = _softmax_parts([_dot_nt(qh, kw) + bias_ref[h], _dot_nt(qh, kc)])
        o_h = (_dot(p_w.astype(BF16), vw) + _dot(p_c.astype(BF16), vc)) * inv
        acc = acc + jnp.where(head == h, o_h, 0.0)
    o_ref[...] = acc.astype(BF16)


def _na_attention(qkv_l, qkv_c, bias):
    nb, seq = qkv_l.shape[0], qkv_l.shape[1]
    tq = NA_QROWS * GRID_W
    nblk = seq // tq
    variant = lambda j: jnp.where(j == 0, 0, jnp.where(j == nblk - 1, 2, 1))
    kern = functools.partial(_na_kernel, grid_rows=seq // GRID_W)
    return pl.pallas_call(
        kern,
        out_shape=jax.ShapeDtypeStruct((nb, seq, 256), BF16),
        grid=(nb, nblk),
        in_specs=[pl.BlockSpec((None, tq, 256), lambda b, j: (b, j, 0)),
                  pl.BlockSpec((None, seq, 256), lambda b, j: (b, 0, 1)),
                  pl.BlockSpec((None, seq, 256), lambda b, j: (b, 0, 2)),
                  pl.BlockSpec((None, CTX_LEN, 256), lambda b, j: (b, 0, 1)),
                  pl.BlockSpec((None, CTX_LEN, 256), lambda b, j: (b, 0, 2)),
                  pl.BlockSpec((None, N_HEADS, tq, NA_KROWS * GRID_W), lambda b, j: (variant(j), 0, 0, 0))],
        out_specs=pl.BlockSpec((None, tq, 256), lambda b, j: (b, j, 0)),
        compiler_params=_params(("parallel", "arbitrary")),
        name="na_attention",
    )(qkv_l, qkv_l, qkv_l, qkv_c, qkv_c, bias)


def _na_ctx_kernel(q_ref, k_ref, v_ref, o_ref):
    q = q_ref[...]
    k = k_ref[...]
    v = v_ref[...]
    head = _lane_group(256, NA_DH)
    acc = jnp.zeros(q.shape, F32)
    for h in range(N_HEADS):
        qh = jnp.where(head == h, q, jnp.zeros_like(q))
        (p,), inv = _softmax_parts([_dot_nt(qh, k)])
        acc = acc + jnp.where(head == h, _dot(p.astype(BF16), v) * inv, 0.0)
    o_ref[...] = acc.astype(BF16)


def _na_ctx_attention(qkv_c):
    nb = qkv_c.shape[0]
    spec = lambda c: pl.BlockSpec((None, CTX_LEN, 256), lambda b: (b, 0, c))
    return pl.pallas_call(
        _na_ctx_kernel,
        out_shape=jax.ShapeDtypeStruct((nb, CTX_LEN, 256), BF16),
        grid=(nb,),
        in_specs=[spec(0), spec(1), spec(2)],
        out_specs=spec(0),
        compiler_params=_params(("parallel",)),
        name="na_ctx_attention",
    )(qkv_c, qkv_c, qkv_c)


def _na_bias_tables(rpb, grid_rows):
    a = np.arange(NA_QROWS)[:, None, None, None]
    qc = np.arange(GRID_W)[None, :, None, None]
    b = np.arange(NA_KROWS)[None, None, :, None]
    kc = np.arange(GRID_W)[None, None, None, :]
    shape = (NA_QROWS, GRID_W, NA_KROWS, GRID_W)
    half = NA_WIN_ROWS // 2
    n_r, n_c = 2 * NA_WIN_ROWS - 1, 2 * NA_WIN_COLS - 1
    cs = np.clip(qc - NA_WIN_COLS // 2, 0, GRID_W - NA_WIN_COLS)
    ci = np.clip(kc - qc + NA_WIN_COLS - 1, 0, n_c - 1)[0, :, 0, :]
    pick_c = (ci[..., None] == np.arange(n_c)).astype(np.float32)
    vis, pick_r = [], []
    for q_row0 in (0, half, grid_rows - NA_QROWS):
        k_row0 = int(np.clip(q_row0 - half, 0, grid_rows - NA_KROWS))
        qr = q_row0 + a
        kr = k_row0 + b
        rs = np.clip(qr - half, 0, grid_rows - NA_WIN_ROWS)
        vis.append(np.broadcast_to((kr >= rs) & (kr < rs + NA_WIN_ROWS) & (kc >= cs) & (kc < cs + NA_WIN_COLS), shape))
        ri = np.clip(kr - qr + NA_WIN_ROWS - 1, 0, n_r - 1)[:, 0, :, 0]
        pick_r.append((ri[..., None] == np.arange(n_r)).astype(np.float32))
    t = jnp.einsum('hrc,vabr,qkc->vhaqbk', rpb, jnp.asarray(np.stack(pick_r)), jnp.asarray(pick_c), precision=HI)
    t = jnp.where(jnp.asarray(np.stack(vis))[:, None], t, NEG)
    return t.reshape(3, N_HEADS, NA_QROWS * GRID_W, NA_KROWS * GRID_W)


DA_KEY_BLOCK = 512


def _slab_reduce(x, op, slab=64):
    folded = op(x.reshape(x.shape[0] // slab, slab, x.shape[1]), axis=0)
    return op(folded, axis=0, keepdims=True)


def _da_kernel(*refs, lam_init, has_lat):
    if has_lat:
        q_ref, kl_ref, vtl_ref, kc_ref, vtc_ref, lam_ref, g_ref, o_ref, st_scr, p_scr = refs
    else:
        q_ref, kc_ref, vtc_ref, lam_ref, g_ref, o_ref = refs
    q = q_ref[...]
    tq = q.shape[0]
    lp = lam_ref[...]
    lam = (jnp.exp(jnp.sum(lp[0:1] * lp[1:2], axis=1, keepdims=True))
           - jnp.exp(jnp.sum(lp[2:3] * lp[3:4], axis=1, keepdims=True)) + lam_init)
    qmap = _lane_group(256, DIFF_DQK)
    zero = jnp.zeros_like(q)
    heads = range(N_HEADS)
    hs = [slice(h * 64, (h + 1) * 64) for h in heads]
    qs = [jnp.concatenate([jnp.where(qmap == 2 * h, q, zero), jnp.where(qmap == 2 * h + 1, q, zero)], axis=0)
          for h in heads]
    kc = kc_ref[...]
    st = [_dot_nt(kc, qs[h]) for h in heads]
    m = [_slab_reduce(st[h], jnp.max) for h in heads]
    p = [jnp.exp2(st[h] - m[h]) for h in heads]
    total = [_slab_reduce(p[h], jnp.sum) for h in heads]
    acc = [_dot(vtc_ref[hs[h], :], p[h].astype(BF16)) for h in heads]
    if has_lat:
        n_blocks = kl_ref.shape[0] // DA_KEY_BLOCK

        def block(i):
            start = i * DA_KEY_BLOCK
            return pl.ds(start if isinstance(i, int) else pl.multiple_of(start, DA_KEY_BLOCK), DA_KEY_BLOCK)

        def scores(i, slot):
            kb = kl_ref[block(i), :]
            for h in heads:
                st_scr[slot, h] = _dot_nt(kb, qs[h])

        def softmax_stage(slot, m, total):
            m_new = [jnp.maximum(m[h], _slab_reduce(st_scr[slot, h], jnp.max)) for h in heads]
            alpha = [jnp.exp2(m[h] - m_new[h]) for h in heads]
            for h in heads:
                p = jnp.exp2(st_scr[slot, h] - m_new[h])
                total[h] = alpha[h] * total[h] + _slab_reduce(p, jnp.sum)
                p_scr[slot, h] = p.astype(BF16)
            return m_new, total, alpha

        def value_stage(acc, alpha, slot, i):
            return [alpha[h] * acc[h] + _dot(vtl_ref[hs[h], block(i)], p_scr[slot, h]) for h in heads]

        def step(i, slot, m, total, acc, alpha_prev):
            scores(jnp.minimum(i + 1, n_blocks - 1), 1 - slot)
            m, total, alpha = softmax_stage(slot, m, list(total))
            acc = value_stage(acc, alpha_prev, 1 - slot, jnp.maximum(i - 1, 0))
            return m, total, acc, alpha

        def body(j, carry):
            m, total, acc, alpha = step(2 * j, 0, *carry)
            return step(2 * j + 1, 1, m, total, acc, alpha)

        scores(0, 0)
        p_scr[1] = jnp.zeros(p_scr.shape[1:], BF16)
        ones = [jnp.ones_like(m[h]) for h in heads]
        m, total, acc, alpha = lax.fori_loop(0, n_blocks // 2, body, (m, total, acc, ones))
        acc = value_stage(acc, alpha, 1, n_blocks - 1)
    outs = []
    for h in heads:
        o_h = acc[h][:, 0:tq] * (1.0 / total[h][:, 0:tq]) - acc[h][:, tq:2 * tq] * (lam / total[h][:, tq:2 * tq])
        y_h = o_h * lax.rsqrt(jnp.mean(o_h * o_h, axis=0, keepdims=True) + EPS) * g_ref[hs[h], :]
        outs.append(y_h * (1.0 - lam_init))
    o_ref[...] = jnp.concatenate(outs, axis=0).T.astype(BF16)


def _da_attention(qkv_q, qkv_l, vt_l, qkv_c, vt_c, lam_params, gain_col, lam_init, tq):
    nb, nq = qkv_q.shape[0], qkv_q.shape[1]
    has_lat = qkv_l is not None
    full = lambda arr, c: pl.BlockSpec((None, arr.shape[1], 256), lambda b, j: (b, 0, c))
    full_t = lambda arr: pl.BlockSpec((None, 256, arr.shape[2]), lambda b, j: (b, 0, 0))
    in_specs = [pl.BlockSpec((None, tq, 256), lambda b, j: (b, j, 0))]
    args = [qkv_q]
    if has_lat:
        in_specs += [full(qkv_l, 1), full_t(vt_l)]
        args += [qkv_l, vt_l]
    in_specs += [full(qkv_c, 1), full_t(vt_c),
                 pl.BlockSpec((4, DIFF_DQK), lambda b, j: (0, 0)), pl.BlockSpec((256, 1), lambda b, j: (0, 0))]
    args += [qkv_c, vt_c, lam_params, gain_col]
    scratch = []
    if has_lat:
        assert (qkv_l.shape[1] // DA_KEY_BLOCK) % 2 == 0
        scratch = [pltpu.VMEM((2, N_HEADS, DA_KEY_BLOCK, 2 * tq), F32),
                   pltpu.VMEM((2, N_HEADS, DA_KEY_BLOCK, 2 * tq), BF16)]
    return pl.pallas_call(
        functools.partial(_da_kernel, lam_init=lam_init, has_lat=has_lat),
        out_shape=jax.ShapeDtypeStruct((nb, nq, 256), BF16),
        grid=(nb, nq // tq),
        in_specs=in_specs,
        out_specs=pl.BlockSpec((None, tq, 256), lambda b, j: (b, j, 0)),
        scratch_shapes=scratch,
        compiler_params=_params(("parallel", "arbitrary")),
        name="diff_attention",
    )(*args)


def _merge_kernel(x_ref, g1_ref, dnf_ref, dnb_ref, z_ref, glf_ref, glb_ref, r_ref, na_ref, df_ref, gate_ref,
                  bgate_ref, dng_ref, glg_ref, wb_ref, wo_ref, o_ref):
    def head_norm(o, gate, g):
        return ((o * lax.rsqrt(_group_mean(o * o, 64) + EPS) * g) * _silu(gate)).astype(BF16)

    ys = (head_norm(dnf_ref[...] + dnb_ref[...], z_ref[...], dng_ref[...]), na_ref[...],
          head_norm(glf_ref[...] + glb_ref[...], r_ref[...], glg_ref[...]), df_ref[...])
    acc = jnp.zeros(x_ref.shape, F32)
    for g in range(N_BRANCH):
        gate = _sigmoid(gate_ref[:, g * D_MODEL:(g + 1) * D_MODEL] + bgate_ref[g:g + 1, :])
        acc = acc + gate * _dot(ys[g], wb_ref[g])
    o_ref[...] = x_ref[...] + g1_ref[...] * _dot(acc.astype(BF16), wo_ref[...])


def _merge(x2, g1, dn_f, dn_b, gl_f, gl_b, y_na, y_df, p, b_gate, dn_g, gl_g, w_branch, w_out, rows_per_mod, tm):
    rows = x2.shape[0]
    row = lambda w, c=0: pl.BlockSpec((tm, w), lambda i: (i, c))
    const = lambda shape: pl.BlockSpec(shape, lambda i: (0,) * len(shape))
    return pl.pallas_call(
        _merge_kernel,
        out_shape=jax.ShapeDtypeStruct((rows, D_MODEL), F32),
        grid=(rows // tm,),
        in_specs=[row(D_MODEL),
                  pl.BlockSpec((None, 1, D_MODEL), lambda i: ((i * tm) // rows_per_mod, 0, 0)),
                  row(256), row(256), row(256, C_DNZ // 256),
                  row(256), row(256), row(256, C_GR // 256),
                  row(256), row(256), row(N_BRANCH * D_MODEL, C_GATE // (N_BRANCH * D_MODEL)),
                  const((N_BRANCH, D_MODEL)), const((1, 256)), const((1, 256)),
                  const((N_BRANCH, BRANCH_W, D_MODEL)), const((D_MODEL, D_MODEL))],
        out_specs=row(D_MODEL),
        compiler_params=_params(("parallel",)),
        name="merge",
    )(x2, g1, dn_f, dn_b, p, gl_f, gl_b, p, y_na, y_df, p, b_gate, dn_g, gl_g, w_branch, w_out)


FFN_COLS = 256


def _ffn_kernel(x_ref, xp_ref, xn_ref, gain_ref, sc_ref, sh_ref, g2_ref, wu_ref, wv_ref, cw_ref, cb_ref, wo_ref,
                o_ref, u_scr, *, tm, seq_len):
    i = pl.program_id(0)
    first = (i * tm) % seq_len == 0
    last = ((i + 1) * tm) % seq_len == 0

    def norm(x):
        y = x * lax.rsqrt(jnp.mean(x * x, axis=-1, keepdims=True) + EPS)
        return (y * gain_ref[...]) * sc_ref[...] + sh_ref[...]

    x = x_ref[...]
    h = norm(x)
    he = jnp.concatenate([norm(xp_ref[...]), h, norm(xn_ref[...])], axis=0).astype(BF16)
    h = h.astype(BF16)
    rows = _iota((tm + 2 * SUBLANES, 1), 0)
    keep = jnp.logical_not((first & (rows < SUBLANES)) | (last & (rows >= tm + SUBLANES)))
    acc = jnp.zeros(x.shape, F32)
    for c in range(D_FF // FFN_COLS):
        cols = slice(c * FFN_COLS, (c + 1) * FFN_COLS)
        u_scr[...] = jnp.where(keep, _dot(he, wu_ref[:, cols]), 0.0)
        conv = (u_scr[SUBLANES - 1:SUBLANES - 1 + tm, :] * cw_ref[0:1, cols]
                + u_scr[SUBLANES:SUBLANES + tm, :] * cw_ref[1:2, cols]
                + u_scr[SUBLANES + 1:SUBLANES + 1 + tm, :] * cw_ref[2:3, cols]) + cb_ref[:, cols]
        act = (_silu(conv) * _dot(h, wv_ref[:, cols])).astype(BF16)
        acc = acc + _dot(act, wo_ref[cols, :])
    o_ref[...] = x + g2_ref[...] * acc


def _ffn(x2, gain, scale1p, shift, g2, w_u, w_v, conv_w, conv_b, w_o, rows_per_mod, seq_len, tm):
    rows = x2.shape[0]
    nb8 = rows // SUBLANES
    t8 = tm // SUBLANES
    const = lambda shape: pl.BlockSpec(shape, lambda i: (0,) * len(shape))
    mod = pl.BlockSpec((None, 1, D_MODEL), lambda i: ((i * tm) // rows_per_mod, 0, 0))
    return pl.pallas_call(
        functools.partial(_ffn_kernel, tm=tm, seq_len=seq_len),
        out_shape=jax.ShapeDtypeStruct((rows, D_MODEL), F32),
        grid=(rows // tm,),
        in_specs=[pl.BlockSpec((tm, D_MODEL), lambda i: (i, 0)),
                  pl.BlockSpec((SUBLANES, D_MODEL), lambda i: (jnp.maximum(i * t8 - 1, 0), 0)),
                  pl.BlockSpec((SUBLANES, D_MODEL), lambda i: (jnp.minimum((i + 1) * t8, nb8 - 1), 0)),
                  const((1, D_MODEL)), mod, mod, mod,
                  const((D_MODEL, D_FF)), const((D_MODEL, D_FF)), const((3, D_FF)), const((1, D_FF)),
                  const((D_FF, D_MODEL))],
        out_specs=pl.BlockSpec((tm, D_MODEL), lambda i: (i, 0)),
        scratch_shapes=[pltpu.VMEM((tm + 2 * SUBLANES, FFN_COLS), F32)],
        compiler_params=_params(("parallel",)),
        name="conv_ffn",
    )(x2, x2, x2, gain, scale1p, shift, g2, w_u, w_v, conv_w, conv_b, w_o)


def _inproj_columns():
    widths = (768, 256, 8, 8, 768, 128, 128, 256, 256, 32, 768, 4096)
    o = np.concatenate([[0], np.cumsum(widths)])
    src = np.zeros(P_WIDTH, np.int64)
    sign = np.zeros(P_WIDTH, np.float32)

    def put(dst, cols, sgn=None):
        src[dst:dst + len(cols)] = cols
        sign[dst:dst + len(cols)] = 1.0 if sgn is None else sgn

    put(C_DN, np.arange(o[0], o[1]))
    put(C_NA, np.arange(o[4], o[5]))
    put(C_DF, np.arange(o[10], o[11]))
    put(C_DNZ, np.arange(o[1], o[2]))
    quarter = DIFF_DQK // 4
    perm = np.concatenate([np.arange(quarter, 2 * quarter), np.arange(0, quarter),
                           np.arange(3 * quarter, 4 * quarter), np.arange(2 * quarter, 3 * quarter)])
    sgn = np.concatenate([-np.ones(quarter), np.ones(quarter), -np.ones(quarter), np.ones(quarter)])
    j = np.arange(512)
    put(C_DFROT, o[10] + (j // DIFF_DQK) * DIFF_DQK + perm[j % DIFF_DQK], sgn[j % DIFF_DQK])
    put(C_GQK, np.arange(o[5], o[7]))
    put(C_GV, np.arange(o[7], o[8]))
    put(C_GR, np.arange(o[8], o[9]))
    put(C_MISC, np.concatenate([np.arange(o[2], o[4]), np.arange(o[9], o[10])]))
    put(C_GATE, np.arange(o[11], o[12]))
    return src, sign, perm


def _rope_tables(length):
    t = jnp.arange(length)
    row = (t // GRID_W).astype(F32)
    col = (t % GRID_W).astype(F32)
    n_freq = DIFF_DQK // 4
    inv_freq = jnp.power(jnp.float32(ROPE_THETA), -jnp.arange(n_freq, dtype=F32) / n_freq)
    ang_r = row[:, None] * inv_freq
    ang_c = col[:, None] * inv_freq
    ang = jnp.concatenate([ang_r, ang_r, ang_c, ang_c], axis=-1)
    reps = 256 // DIFF_DQK
    return jnp.tile(jnp.cos(ang), (1, reps)), jnp.tile(jnp.sin(ang), (1, reps))


def _pad_lanes(v, offset, width):
    return jnp.zeros((1, width), F32).at[0, offset:offset + v.shape[0]].set(v)


def kernel(x, c, ctx, c_ctx, w_mod, b_mod, norm1_g, norm2_g, w_in, b_gate, dn_conv, dn_a_log, dn_dt_bias, dn_norm_g,
           na_q_norm, na_k_norm, na_rpb, gla_w_a2, gla_b_a, gla_norm_g, df_q_norm, df_k_norm, df_lambda, df_norm_g,
           w_branch, w_out, ffn_w_in, ffn_conv_w, ffn_conv_b, ffn_w_out):
    nb, seq, _ = x.shape
    n_lat, n_ctx = nb * seq, nb * CTX_LEN
    tm_lat, tm_ctx = 512, 256

    cc = jnp.zeros((16, D_MODEL), F32).at[0:nb].set(c).at[nb].set(c_ctx)
    mod = _modulation(cc, w_mod, b_mod)

    src, sign, perm = _inproj_columns()
    w_in_p = (jnp.take(w_in, jnp.asarray(src), axis=2) * jnp.asarray(sign)).astype(BF16)
    w_branch_b = w_branch.astype(BF16)
    w_out_b = w_out.astype(BF16)
    w_u = ffn_w_in[:, :, 0:D_FF].astype(BF16)
    w_v = ffn_w_in[:, :, D_FF:2 * D_FF].astype(BF16)
    w_o = ffn_w_out.astype(BF16)
    cos, sin = _rope_tables(seq)

    x2 = x.reshape(n_lat, D_MODEL)
    xc2 = ctx.reshape(n_ctx, D_MODEL)
    for li in range(DEPTH):
        with_ctx = li < DEPTH - 1
        lam_init = 0.8 - 0.6 * math.exp(-0.3 * li)
        m = mod[li]
        part = lambda k: m[:, k * D_MODEL:(k + 1) * D_MODEL]
        lat = lambda v: v[0:nb].reshape(nb, 1, D_MODEL)
        cx = lambda v: v[nb:nb + 1].reshape(1, 1, D_MODEL)
        sh1, sc1, g1, sh2, sc2, g2 = (part(k) for k in range(6))
        n1 = norm1_g[li].reshape(1, D_MODEL)
        n2 = norm2_g[li].reshape(1, D_MODEL)

        p_l = _inproj(x2, n1, lat(1.0 + sc1), lat(sh1), w_in_p[li], seq, 1024)
        p_c = _inproj(xc2, n1, cx(1.0 + sc1), cx(sh1), w_in_p[li], n_ctx, 1024)

        tile4 = lambda v: jnp.tile(v, 256 // v.shape[0]).reshape(1, 256)
        vecs = (dn_conv[li], _pad_lanes(dn_a_log[li].reshape(-1), 2 * N_HEADS, LANES),
                _pad_lanes(dn_dt_bias[li].reshape(-1), 2 * N_HEADS, LANES),
                tile4(na_q_norm[li]), tile4(na_k_norm[li]), tile4(df_q_norm[li]), tile4(df_k_norm[li]),
                tile4(df_q_norm[li][perm]), tile4(df_k_norm[li][perm]))
        dn_l, bg_l, na_l, df_l = _prep(p_l, cos, sin, vecs, tp=tm_lat, seq_len=seq, rope=True)
        dn_c, bg_c, na_c, df_c = _prep(p_c, cos, sin, vecs, tp=tm_ctx, seq_len=CTX_LEN, rope=False)

        b3 = lambda a, t: a.reshape(nb, t, a.shape[-1])
        dn_cf, dn_cb, dn_lf, dn_lb = _dn_scan(b3(dn_c, CTX_LEN), b3(bg_c, CTX_LEN), b3(dn_l, seq), b3(bg_l, seq))

        w2 = jnp.zeros((LANES, 256), F32)
        a1_off = 4 * N_HEADS
        w2 = w2.at[a1_off:a1_off + GLA_RANK, 0:128].set(gla_w_a2[li, 0])
        w2 = w2.at[a1_off + GLA_RANK:a1_off + 2 * GLA_RANK, 128:256].set(gla_w_a2[li, 1])
        gl_cf, gl_cb, gl_lf, gl_lb = _gla_scan(b3(p_c, CTX_LEN), b3(p_l, seq), w2, gla_b_a[li].reshape(1, 256))

        y_na = _na_attention(b3(na_l, seq), b3(na_c, CTX_LEN), _na_bias_tables(na_rpb[li], seq // GRID_W))
        vt_l = jnp.swapaxes(b3(df_l, seq)[:, :, 512:768], 1, 2)
        vt_c = jnp.swapaxes(b3(df_c, CTX_LEN)[:, :, 512:768], 1, 2)
        df_gain = tile4(df_norm_g[li]).reshape(256, 1)
        y_df = _da_attention(b3(df_l, seq), b3(df_l, seq), vt_l, b3(df_c, CTX_LEN), vt_c, df_lambda[li],
                             df_gain, lam_init, 128)

        flat = lambda a: a.reshape(-1, a.shape[-1])
        gains = (b_gate[li], tile4(dn_norm_g[li]), tile4(gla_norm_g[li]), w_branch_b[li], w_out_b[li])
        ffn_w = (w_u[li], w_v[li], ffn_conv_w[li], ffn_conv_b[li].reshape(1, D_FF), w_o[li])
        x2 = _merge(x2, lat(g1), flat(dn_lf), flat(dn_lb), flat(gl_lf), flat(gl_lb), flat(y_na), flat(y_df), p_l,
                    *gains, seq, tm_lat)
        x2 = _ffn(x2, n2, lat(1.0 + sc2), lat(sh2), lat(g2), *ffn_w, seq, seq, tm_lat)
        if with_ctx:
            yc_na = _na_ctx_attention(b3(na_c, CTX_LEN))
            yc_df = _da_attention(b3(df_c, CTX_LEN), None, None, b3(df_c, CTX_LEN), vt_c, df_lambda[li],
                                  df_gain, lam_init, 128)
            xc2 = _merge(xc2, cx(g1), flat(dn_cf), flat(dn_cb), flat(gl_cf), flat(gl_cb), flat(yc_na), flat(yc_df),
                         p_c, *gains, n_ctx, tm_ctx)
            xc2 = _ffn(xc2, n2, cx(1.0 + sc2), cx(sh2), cx(g2), *ffn_w, n_ctx, CTX_LEN, tm_ctx)
    return x2.reshape(nb, seq, D_MODEL)
```

```python
import functools
import math

import numpy as np
import jax
import jax.numpy as jnp
from jax import lax
from jax.experimental import pallas as pl
from jax.experimental.pallas import tpu as pltpu

F32 = jnp.float32
BF16 = jnp.bfloat16
HI = lax.Precision.HIGHEST

D_MODEL = 1024
DEPTH = 4
GRID_W = 64
CTX_LEN = 256
N_BRANCH = 4
BRANCH_W = 256
CHUNK = 64
N_HEADS = 4
DN_DK = 64
DN_CONV = 3
NA_DH = 64
NA_WIN_ROWS = 8
NA_WIN_COLS = 16
GLA_DK = 32
GLA_RANK = 16
GLA_TAU = 16.0
DIFF_DQK = 32
ROPE_THETA = 10000.0
D_FF = 2816
EPS = 1e-6
NEG = -1e30
LOG2E = 1.4426950408889634

LANES = 128
SUBLANES = 8
HALO = 16
VMEM_LIMIT = 56 * 1024 * 1024

C_DN = 0
C_NA = 768
C_DF = 1536
C_DNZ = 2304
C_DFROT = 2560
C_GQK = 3072
C_GV = 3328
C_GR = 3584
C_MISC = 3840
C_GATE = 4096
P_WIDTH = 8192

GROUP = 4 * CHUNK
NA_QROWS = 4
NA_KROWS = 12


def _dot(a, b, prec=None):
    return jnp.dot(a, b, preferred_element_type=F32, precision=prec)


def _dot_nt(a, b, prec=None):
    return lax.dot_general(a, b, (((1,), (1,)), ((), ())), preferred_element_type=F32, precision=prec)


def _dot_tn(a, b, prec=None):
    return lax.dot_general(a, b, (((0,), (0,)), ((), ())), preferred_element_type=F32, precision=prec)


def _iota(shape, dim):
    return lax.broadcasted_iota(jnp.int32, shape, dim)


def _group_mask(rows, cols, rgroup, cgroup):
    return (_iota((rows, cols), 0) // rgroup == _iota((rows, cols), 1) // cgroup).astype(F32)


def _group_mean(xsq, width):
    n = xsq.shape[-1]
    ones = _group_mask(n, n, width, width)
    return _dot(xsq, ones, HI) * (1.0 / width)


def _sigmoid(x):
    return 0.5 * jnp.tanh(0.5 * x) + 0.5


def _silu(x):
    return x * _sigmoid(x)


def _softplus(x):
    return jnp.maximum(x, 0.0) + jnp.log1p(jnp.exp(-jnp.abs(x)))


def _params(sem):
    return pltpu.CompilerParams(dimension_semantics=sem, vmem_limit_bytes=VMEM_LIMIT)


def _mod_kernel(c_ref, w_ref, b_ref, o_ref):
    o_ref[...] = _dot(_silu(c_ref[...]), w_ref[...], HI) + b_ref[...]


def _modulation(cc, w_mod, b_mod):
    tn = 1024
    return pl.pallas_call(
        _mod_kernel,
        out_shape=jax.ShapeDtypeStruct((DEPTH, 16, 6 * D_MODEL), F32),
        grid=(DEPTH, 6 * D_MODEL // tn),
        in_specs=[pl.BlockSpec((16, D_MODEL), lambda l, j: (0, 0)),
                  pl.BlockSpec((None, D_MODEL, tn), lambda l, j: (l, 0, j)),
                  pl.BlockSpec((None, 1, tn), lambda l, j: (l, 0, j))],
        out_specs=pl.BlockSpec((None, 16, tn), lambda l, j: (l, 0, j)),
        compiler_params=_params(("arbitrary", "arbitrary")),
        name="modulation",
    )(cc, w_mod, b_mod.reshape(DEPTH, 1, 6 * D_MODEL))


def _inproj_kernel(x_ref, g_ref, sc_ref, sh_ref, w_ref, o_ref, misc_ref, h_scr, *, tn):
    j = pl.program_id(1)

    @pl.when(j == 0)
    def _():
        x = x_ref[...]
        y = x * lax.rsqrt(jnp.mean(x * x, axis=-1, keepdims=True) + EPS)
        h_scr[...] = ((y * g_ref[...]) * sc_ref[...] + sh_ref[...]).astype(BF16)

    acc = _dot(h_scr[...], w_ref[...])
    o_ref[...] = acc.astype(BF16)

    @pl.when(j == C_MISC // tn)
    def _():
        misc_ref[...] = acc[:, C_MISC % tn:C_MISC % tn + LANES]


def _inproj(x2, gain, scale1p, shift, w, rows_per_mod, tm):
    rows = x2.shape[0]
    tn = 1024
    mod_spec = pl.BlockSpec((None, 1, D_MODEL), lambda i, j: ((i * tm) // rows_per_mod, 0, 0))
    return pl.pallas_call(
        functools.partial(_inproj_kernel, tn=tn),
        out_shape=(jax.ShapeDtypeStruct((rows, P_WIDTH), BF16), jax.ShapeDtypeStruct((rows, LANES), F32)),
        grid=(rows // tm, P_WIDTH // tn),
        in_specs=[pl.BlockSpec((tm, D_MODEL), lambda i, j: (i, 0)),
                  pl.BlockSpec((1, D_MODEL), lambda i, j: (0, 0)),
                  mod_spec, mod_spec,
                  pl.BlockSpec((D_MODEL, tn), lambda i, j: (0, j))],
        out_specs=(pl.BlockSpec((tm, tn), lambda i, j: (i, j)), pl.BlockSpec((tm, LANES), lambda i, j: (i, 0))),
        scratch_shapes=[pltpu.VMEM((tm, D_MODEL), BF16)],
        compiler_params=_params(("parallel", "arbitrary")),
        name="inproj",
    )(x2, gain, scale1p, shift, w)


def _prep_kernel(dn_ref, dnp_ref, dnn_ref, misc_ref, na_ref, df_ref, dfrot_ref, cos_ref, sin_ref,
                 convw_ref, alog_ref, dt_ref, naq_ref, nak_ref, dfq_ref, dfk_ref, dfqr_ref, dfkr_ref,
                 dn_o, bg_o, na_o, df_o, xe_scr, *, tp, seq_len, rope):
    i = pl.program_id(0)
    first = (i * tp) % seq_len == 0
    last = ((i + 1) * tp) % seq_len == 0
    halo_prev = dnp_ref[...].astype(F32)[HALO - SUBLANES:HALO]
    halo_next = dnn_ref[...].astype(F32)[0:SUBLANES]
    xe_scr[0:SUBLANES, :] = jnp.where(first, 0.0, halo_prev)
    xe_scr[SUBLANES:SUBLANES + tp, :] = dn_ref[...].astype(F32)
    xe_scr[SUBLANES + tp:2 * SUBLANES + tp, :] = jnp.where(last, 0.0, halo_next)
    cw = convw_ref[...]
    conv = (xe_scr[SUBLANES - 1:SUBLANES - 1 + tp, :] * cw[0:1]
            + xe_scr[SUBLANES:SUBLANES + tp, :] * cw[1:2]
            + xe_scr[SUBLANES + 1:SUBLANES + 1 + tp, :] * cw[2:3])
    s = _silu(conv)
    q, k, v = s[:, 0:256], s[:, 256:512], s[:, 512:768]
    qn = q * lax.rsqrt(_group_mean(q * q, DN_DK) * DN_DK + EPS) * (DN_DK ** -0.5)
    kn = k * lax.rsqrt(_group_mean(k * k, DN_DK) * DN_DK + EPS)
    dn_o[:, 0:256] = qn
    dn_o[:, 256:512] = kn
    dn_o[:, 512:768] = v
    m = misc_ref[...]
    beta = _sigmoid(m)
    g = -jnp.exp(alog_ref[...]) * _softplus(m + dt_ref[...])
    bg_o[...] = jnp.where(_iota(m.shape, 1) < 2 * N_HEADS, beta, g)
    q, k = na_ref[:, 0:256].astype(F32), na_ref[:, 256:512].astype(F32)
    qn = q * lax.rsqrt(_group_mean(q * q, NA_DH) + EPS) * naq_ref[...] * (NA_DH ** -0.5)
    kn = k * lax.rsqrt(_group_mean(k * k, NA_DH) + EPS) * nak_ref[...]
    na_o[:, 0:256] = qn.astype(BF16)
    na_o[:, 256:512] = kn.astype(BF16)
    na_o[:, 512:768] = na_ref[:, 512:768]
    q, k = df_ref[:, 0:256].astype(F32), df_ref[:, 256:512].astype(F32)
    rq = lax.rsqrt(_group_mean(q * q, DIFF_DQK) + EPS)
    rk = lax.rsqrt(_group_mean(k * k, DIFF_DQK) + EPS)
    qn = q * rq * dfq_ref[...]
    kn = k * rk * dfk_ref[...]
    if rope:
        xr = dfrot_ref[...].astype(F32)
        cos, sin = cos_ref[...], sin_ref[...]
        qn = qn * cos + (xr[:, 0:256] * rq * dfqr_ref[...]) * sin
        kn = kn * cos + (xr[:, 256:512] * rk * dfkr_ref[...]) * sin
    df_o[:, 0:256] = (qn * (DIFF_DQK ** -0.5 * LOG2E)).astype(BF16)
    df_o[:, 256:512] = kn.astype(BF16)
    df_o[:, 512:768] = df_ref[:, 512:768]


def _prep(p, misc, cos, sin, vecs, *, tp, seq_len, rope):
    rows = p.shape[0]
    nb8 = rows // HALO
    t8 = tp // HALO
    nseq_tiles = max(seq_len // tp, 1)
    row_vec = lambda w: pl.BlockSpec((1, w), lambda i: (0, 0))
    kern = functools.partial(_prep_kernel, tp=tp, seq_len=seq_len, rope=rope)
    return pl.pallas_call(
        kern,
        out_shape=(jax.ShapeDtypeStruct((rows, 768), F32), jax.ShapeDtypeStruct((rows, LANES), F32),
                   jax.ShapeDtypeStruct((rows, 768), BF16), jax.ShapeDtypeStruct((rows, 768), BF16)),
        grid=(rows // tp,),
        in_specs=[pl.BlockSpec((tp, 768), lambda i: (i, C_DN // 768)),
                  pl.BlockSpec((HALO, 768), lambda i: (jnp.maximum(i * t8 - 1, 0), C_DN // 768)),
                  pl.BlockSpec((HALO, 768), lambda i: (jnp.minimum((i + 1) * t8, nb8 - 1), C_DN // 768)),
                  pl.BlockSpec((tp, LANES), lambda i: (i, 0)),
                  pl.BlockSpec((tp, 768), lambda i: (i, C_NA // 768)),
                  pl.BlockSpec((tp, 768), lambda i: (i, C_DF // 768)),
                  pl.BlockSpec((tp, 512), lambda i: (i, C_DFROT // 512)),
                  pl.BlockSpec((tp, 256), lambda i: (i % nseq_tiles, 0)),
                  pl.BlockSpec((tp, 256), lambda i: (i % nseq_tiles, 0)),
                  pl.BlockSpec((DN_CONV, 768), lambda i: (0, 0)),
                  row_vec(LANES), row_vec(LANES),
                  row_vec(256), row_vec(256), row_vec(256), row_vec(256), row_vec(256), row_vec(256)],
        out_specs=(pl.BlockSpec((tp, 768), lambda i: (i, 0)), pl.BlockSpec((tp, LANES), lambda i: (i, 0)),
                   pl.BlockSpec((tp, 768), lambda i: (i, 0)), pl.BlockSpec((tp, 768), lambda i: (i, 0))),
        scratch_shapes=[pltpu.VMEM((tp + 2 * SUBLANES, 768), F32)],
        compiler_params=_params(("parallel",)),
        name="prep",
    )(p, p, p, misc, p, p, p, cos, sin, *vecs)


def _mm(a, b):
    return _dot(a.astype(BF16), b.astype(BF16))


def _mm_nt(a, b):
    return _dot_nt(a.astype(BF16), b.astype(BF16))


def _mm_tn(a, b):
    return _dot_tn(a.astype(BF16), b.astype(BF16))


def _tile4(x):
    return jnp.concatenate([x, x, x, x], axis=0)


def _tri_consts(d, width):
    rows = _iota((CHUNK, width), 0)
    lane_j = _iota((CHUNK, width), 1) % CHUNK
    incl = (lane_j <= rows) if d == 0 else (lane_j >= rows)
    strict = (lane_j < rows) if d == 0 else (lane_j > rows)
    eye_cat = (lane_j == rows).astype(F32)
    r64 = _iota((CHUNK, CHUNK), 0)
    c64 = _iota((CHUNK, CHUNK), 1)
    cum_mat = ((c64 <= r64) if d == 0 else (c64 >= r64)).astype(F32)
    return incl, strict, eye_cat, cum_mat


def _dn_group(qkv_f, bg_f, qkv_b, bg_b, of_ref, ob_ref, sf_scr, sb_scr):
    t = qkv_f.shape[0]
    n = t // CHUNK
    rows = _iota((CHUNK, 256), 0)
    lane_j = _iota((CHUNK, 256), 1) % CHUNK
    eye_cat = (lane_j == rows).astype(F32)
    mask_bd = _iota((256, 256), 0) // CHUNK == _iota((256, 256), 1) // CHUNK
    zero_b = jnp.zeros((), BF16)

    def bd(x):
        return jnp.where(mask_bd, _tile4(x.astype(BF16)), zero_b)

    work = []
    for d, (qkv, bg) in enumerate(((qkv_f, bg_f), (qkv_b, bg_b))):
        incl = (lane_j <= rows) if d == 0 else (lane_j >= rows)
        strict = (lane_j < rows) if d == 0 else (lane_j > rows)
        srow = _iota((LANES, 512), 0)
        scol = _iota((LANES, 512), 1)
        sel = (srow == (scol // 256) * 2 * N_HEADS + d * N_HEADS + (scol % 256) // CHUNK).astype(F32)
        e = _dot(bg[...], sel, HI)
        rt = _iota((t, t), 0)
        ct = _iota((t, t), 1)
        cum = ((rt // CHUNK == ct // CHUNK) & ((ct <= rt) if d == 0 else (ct >= rt))).astype(F32)
        gc_all = _dot(cum, e[:, 256:512], HI)
        for c in (range(n) if d == 0 else range(n - 1, -1, -1)):
            sl = slice(c * CHUNK, (c + 1) * CHUNK)
            q, k, v = qkv[sl, 0:256], qkv[sl, 256:512], qkv[sl, 512:768]
            beta, gc = e[sl, 0:256], gc_all[sl]
            gc_row = jnp.sum(eye_cat * gc, axis=0, keepdims=True)
            decay = jnp.exp(jnp.where(incl, gc - gc_row, -jnp.inf))
            gc_last = gc[CHUNK - 1:CHUNK] if d == 0 else gc[0:1]
            kb = k * beta
            gram = _dot_nt(jnp.concatenate([kb, q], axis=0).astype(BF16), bd(k))
            n_cat = jnp.where(strict, gram[0:CHUNK] * decay, 0.0)
            egc = jnp.exp(gc)
            work.append(dict(
                d=d, sl=sl, n_cat=n_cat, a_cat=jnp.where(incl, gram[CHUNK:2 * CHUNK] * decay, 0.0).astype(BF16),
                rhs_u=bd(v * beta), rhs_w=bd(kb * egc), qd=(q * egc).astype(BF16),
                kd=(k * jnp.exp(gc_last - gc)).astype(BF16), s_decay=jnp.exp(gc_last),
                p=eye_cat - jnp.where(rows // 2 == lane_j // 2, n_cat, 0.0)))
    size = 2
    while size < CHUNK:
        off = (rows // (2 * size) == lane_j // (2 * size)) & (rows // size != lane_j // size)
        half = [_dot(w["p"].astype(BF16), bd(jnp.where(off, w["n_cat"], 0.0))) for w in work]
        for w, hx in zip(work, half):
            w["p"] = w["p"] - _dot(hx.astype(BF16), bd(w["p"]))
        size *= 2
    for w in work:
        pb = w["p"].astype(BF16)
        w["u"] = _dot(pb, w["rhs_u"])
        w["w"] = _dot(pb, w["rhs_w"]).astype(BF16)
    state = [sf_scr[...], sb_scr[...]]
    out = (of_ref, ob_ref)
    for c in range(n):
        for d in range(2):
            w = work[d * n + c]
            sb = state[d].astype(BF16)
            v_new = w["u"] - _dot(w["w"], sb)
            out[d][w["sl"], :] = _dot(w["qd"], sb) + _dot(w["a_cat"], bd(v_new))
            upd = _dot_tn(w["kd"], v_new.astype(BF16))
            state[d] = state[d] * w["s_decay"] + jnp.where(mask_bd, upd, 0.0)
    sf_scr[...] = state[0]
    sb_scr[...] = state[1]


def _dn_kernel(cq, cbg, lqf, lbgf, lqb, lbgb, ocf, ocb, olf, olb, sf_scr, sb_scr):
    step = pl.program_id(1)

    @pl.when(step == 0)
    def _():
        sf_scr[...] = jnp.zeros_like(sf_scr)
        sb_scr[...] = jnp.zeros_like(sb_scr)
        _dn_group(cq, cbg, cq, cbg, ocf, ocb, sf_scr, sb_scr)

    @pl.when(step > 0)
    def _():
        _dn_group(lqf, lbgf, lqb, lbgb, olf, olb, sf_scr, sb_scr)


def _scan_specs(width, ngroups):
    ctx = pl.BlockSpec((None, CTX_LEN, width), lambda b, s: (b, 0, 0))
    fwd = pl.BlockSpec((None, GROUP, width), lambda b, s: (b, jnp.maximum(s - 1, 0), 0))
    bwd = pl.BlockSpec((None, GROUP, width), lambda b, s: (b, ngroups - 1 - jnp.maximum(s - 1, 0), 0))
    return ctx, fwd, bwd


def _dn_scan(qkv_c, bg_c, qkv_l, bg_l):
    nb, seq = qkv_l.shape[0], qkv_l.shape[1]
    ng = seq // GROUP
    c768, f768, b768 = _scan_specs(768, ng)
    c128, f128, b128 = _scan_specs(LANES, ng)
    c256, f256, b256 = _scan_specs(256, ng)
    return pl.pallas_call(
        _dn_kernel,
        out_shape=(jax.ShapeDtypeStruct((nb, CTX_LEN, 256), F32), jax.ShapeDtypeStruct((nb, CTX_LEN, 256), F32),
                   jax.ShapeDtypeStruct((nb, seq, 256), F32), jax.ShapeDtypeStruct((nb, seq, 256), F32)),
        grid=(nb, ng + 1),
        in_specs=[c768, c128, f768, f128, b768, b128],
        out_specs=(c256, c256, f256, b256),
        scratch_shapes=[pltpu.VMEM((256, 256), F32), pltpu.VMEM((256, 256), F32)],
        compiler_params=_params(("parallel", "arbitrary")),
        name="dn_scan",
    )(qkv_c, bg_c, qkv_l, bg_l, qkv_l, bg_l)


def _gla_group(qk_f, v_f, m_f, qk_b, v_b, m_b, of_ref, ob_ref, sf_scr, sb_scr, w2, ba):
    t = qk_f.shape[0]
    n = t // CHUNK
    rows = _iota((CHUNK, 256), 0)
    lane_j = _iota((CHUNK, 256), 1) % CHUNK
    mask_k = _iota((256, 128), 0) // CHUNK == _iota((256, 128), 1) // GLA_DK
    mask_v = _iota((256, 256), 0) // CHUNK == _iota((256, 256), 1) // CHUNK
    mask_s = _iota((128, 256), 0) // GLA_DK == _iota((128, 256), 1) // CHUNK
    eye = (_iota((128, 128), 0) == _iota((128, 128), 1)).astype(F32)
    zero_b = jnp.zeros((), BF16)
    work = []
    for d, (qk, v, misc) in enumerate(((qk_f, v_f, m_f), (qk_b, v_b, m_b))):
        incl = (lane_j <= rows) if d == 0 else (lane_j >= rows)
        logit = _dot(misc[...], w2[:, d * 128:(d + 1) * 128], HI) + ba[:, d * 128:(d + 1) * 128]
        log_a = (jnp.minimum(logit, 0.0) - jnp.log1p(jnp.exp(-jnp.abs(logit)))) * (1.0 / GLA_TAU)
        rt = _iota((t, t), 0)
        ct = _iota((t, t), 1)
        cum = ((rt // CHUNK == ct // CHUNK) & ((ct <= rt) if d == 0 else (ct >= rt))).astype(F32)
        b_all = _dot(cum, log_a, HI)
        mid = CHUNK // 2 - 1 if d == 0 else CHUNK // 2
        last = CHUNK - 1 if d == 0 else 0
        for c in (range(n) if d == 0 else range(n - 1, -1, -1)):
            sl = slice(c * CHUNK, (c + 1) * CHUNK)
            q = qk[sl, 0:128].astype(F32) * (GLA_DK ** -0.5)
            k = qk[sl, 128:256].astype(F32)
            vb = v[sl, :]
            b = b_all[sl]
            b_mid = b[mid:mid + 1]
            b_last = b[last:last + 1]
            ke = jnp.where(mask_k, _tile4((k * jnp.exp(b_mid - b)).astype(BF16)), zero_b)
            a = jnp.where(incl, _dot_nt((q * jnp.exp(b - b_mid)).astype(BF16), ke), 0.0)
            o_intra = _dot(a.astype(BF16), jnp.where(mask_v, _tile4(vb), zero_b))
            ds = jnp.where(mask_s, _dot_tn((k * jnp.exp(b_last - b)).astype(BF16), vb), 0.0)
            dec_col = jnp.sum(eye * jnp.exp(b_last), axis=1, keepdims=True)
            work.append(dict(sl=sl, o_intra=o_intra, qb=(q * jnp.exp(b)).astype(BF16), ds=ds, dec_col=dec_col))
    state = [sf_scr[...], sb_scr[...]]
    out = (of_ref, ob_ref)
    for c in range(n):
        for d in range(2):
            w = work[d * n + c]
            out[d][w["sl"], :] = w["o_intra"] + _dot(w["qb"], state[d].astype(BF16))
            state[d] = w["dec_col"] * state[d] + w["ds"]
    sf_scr[...] = state[0]
    sb_scr[...] = state[1]


def _gla_kernel(cqk, cv, cm, lqkf, lvf, lmf, lqkb, lvb, lmb, w2_ref, ba_ref,
                ocf, ocb, olf, olb, sf_scr, sb_scr):
    step = pl.program_id(1)
    w2 = w2_ref[...]
    ba = ba_ref[...]

    @pl.when(step == 0)
    def _():
        sf_scr[...] = jnp.zeros_like(sf_scr)
        sb_scr[...] = jnp.zeros_like(sb_scr)
        _gla_group(cqk, cv, cm, cqk, cv, cm, ocf, ocb, sf_scr, sb_scr, w2, ba)

    @pl.when(step > 0)
    def _():
        _gla_group(lqkf, lvf, lmf, lqkb, lvb, lmb, olf, olb, sf_scr, sb_scr, w2, ba)


def _col_specs(width, col, ngroups):
    blk = col // width
    ctx = pl.BlockSpec((None, CTX_LEN, width), lambda b, s: (b, 0, blk))
    fwd = pl.BlockSpec((None, GROUP, width), lambda b, s: (b, jnp.maximum(s - 1, 0), blk))
    bwd = pl.BlockSpec((None, GROUP, width), lambda b, s: (b, ngroups - 1 - jnp.maximum(s - 1, 0), blk))
    return ctx, fwd, bwd


def _gla_scan(p_c, misc_c, p_l, misc_l, w2, ba):
    nb, seq = p_l.shape[0], p_l.shape[1]
    ng = seq // GROUP
    cqk, fqk, bqk = _col_specs(256, C_GQK, ng)
    cv, fv, bv = _col_specs(256, C_GV, ng)
    cm, fm, bm = _scan_specs(LANES, ng)
    c256, f256, b256 = _scan_specs(256, ng)
    const = lambda shape: pl.BlockSpec(shape, lambda b, s: (0, 0))
    return pl.pallas_call(
        _gla_kernel,
        out_shape=(jax.ShapeDtypeStruct((nb, CTX_LEN, 256), F32), jax.ShapeDtypeStruct((nb, CTX_LEN, 256), F32),
                   jax.ShapeDtypeStruct((nb, seq, 256), F32), jax.ShapeDtypeStruct((nb, seq, 256), F32)),
        grid=(nb, ng + 1),
        in_specs=[cqk, cv, cm, fqk, fv, fm, bqk, bv, bm, const((LANES, 256)), const((1, 256))],
        out_specs=(c256, c256, f256, b256),
        scratch_shapes=[pltpu.VMEM((128, 256), F32), pltpu.VMEM((128, 256), F32)],
        compiler_params=_params(("parallel", "arbitrary")),
        name="gla_scan",
    )(p_c, p_c, misc_c, p_l, p_l, misc_l, p_l, p_l, misc_l, w2, ba)


def _lane_group(width, group):
    return _iota((1, width), 1) // group


def _softmax_parts(parts):
    m = parts[0].max(axis=-1, keepdims=True)
    for s in parts[1:]:
        m = jnp.maximum(m, s.max(axis=-1, keepdims=True))
    ps = [jnp.exp(s - m) for s in parts]
    total = ps[0].sum(axis=-1, keepdims=True)
    for p in ps[1:]:
        total = total + p.sum(axis=-1, keepdims=True)
    return ps, 1.0 / total


def _na_kernel(q_ref, kl_ref, vl_ref, kc_ref, vc_ref, bias_ref, o_ref, *, grid_rows):
    blk = pl.program_id(1)
    key_row0 = jnp.clip(blk * NA_QROWS - NA_WIN_ROWS // 2, 0, grid_rows - NA_KROWS)
    start = pl.multiple_of(key_row0 * GRID_W, GRID_W)
    kw = kl_ref[pl.ds(start, NA_KROWS * GRID_W), :]
    vw = vl_ref[pl.ds(start, NA_KROWS * GRID_W), :]
    kc = kc_ref[...]
    vc = vc_ref[...]
    q = q_ref[...]
    head = _lane_group(256, NA_DH)
    acc = jnp.zeros(q.shape, F32)
    for h in range(N_HEADS):
        qh = jnp.where(head == h, q, jnp.zeros_like(q))
        (p_w, p_c), inv = _softmax_parts([_dot_nt(qh, kw) + bias_ref[h], _dot_nt(qh, kc)])
        o_h = (_dot(p_w.astype(BF16), vw) + _dot(p_c.astype(BF16), vc)) * inv
        acc = acc + jnp.where(head == h, o_h, 0.0)
    o_ref[...] = acc.astype(BF16)


def _na_attention(qkv_l, qkv_c, bias):
    nb, seq = qkv_l.shape[0], qkv_l.shape[1]
    tq = NA_QROWS * GRID_W
    nblk = seq // tq
    variant = lambda j: jnp.where(j == 0, 0, jnp.where(j == nblk - 1, 2, 1))
    kern = functools.partial(_na_kernel, grid_rows=seq // GRID_W)
    return pl.pallas_call(
        kern,
        out_shape=jax.ShapeDtypeStruct((nb, seq, 256), BF16),
        grid=(nb, nblk),
        in_specs=[pl.BlockSpec((None, tq, 256), lambda b, j: (b, j, 0)),
                  pl.BlockSpec((None, seq, 256), lambda b, j: (b, 0, 1)),
                  pl.BlockSpec((None, seq, 256), lambda b, j: (b, 0, 2)),
                  pl.BlockSpec((None, CTX_LEN, 256), lambda b, j: (b, 0, 1)),
                  pl.BlockSpec((None, CTX_LEN, 256), lambda b, j: (b, 0, 2)),
                  pl.BlockSpec((None, N_HEADS, tq, NA_KROWS * GRID_W), lambda b, j: (variant(j), 0, 0, 0))],
        out_specs=pl.BlockSpec((None, tq, 256), lambda b, j: (b, j, 0)),
        compiler_params=_params(("parallel", "arbitrary")),
        name="na_attention",
    )(qkv_l, qkv_l, qkv_l, qkv_c, qkv_c, bias)


def _na_ctx_kernel(q_ref, k_ref, v_ref, o_ref):
    q = q_ref[...]
    k = k_ref[...]
    v = v_ref[...]
    head = _lane_group(256, NA_DH)
    acc = jnp.zeros(q.shape, F32)
    for h in range(N_HEADS):
        qh = jnp.where(head == h, q, jnp.zeros_like(q))
        (p,), inv = _softmax_parts([_dot_nt(qh, k)])
        acc = acc + jnp.where(head == h, _dot(p.astype(BF16), v) * inv, 0.0)
    o_ref[...] = acc.astype(BF16)


def _na_ctx_attention(qkv_c):
    nb = qkv_c.shape[0]
    spec = lambda c: pl.BlockSpec((None, CTX_LEN, 256), lambda b: (b, 0, c))
    return pl.pallas_call(
        _na_ctx_kernel,
        out_shape=jax.ShapeDtypeStruct((nb, CTX_LEN, 256), BF16),
        grid=(nb,),
        in_specs=[spec(0), spec(1), spec(2)],
        out_specs=spec(0),
        compiler_params=_params(("parallel",)),
        name="na_ctx_attention",
    )(qkv_c, qkv_c, qkv_c)


def _na_bias_tables(rpb, grid_rows):
    a = np.arange(NA_QROWS)[:, None, None, None]
    qc = np.arange(GRID_W)[None, :, None, None]
    b = np.arange(NA_KROWS)[None, None, :, None]
    kc = np.arange(GRID_W)[None, None, None, :]
    shape = (NA_QROWS, GRID_W, NA_KROWS, GRID_W)
    half = NA_WIN_ROWS // 2
    n_r, n_c = 2 * NA_WIN_ROWS - 1, 2 * NA_WIN_COLS - 1
    cs = np.clip(qc - NA_WIN_COLS // 2, 0, GRID_W - NA_WIN_COLS)
    ci = np.clip(kc - qc + NA_WIN_COLS - 1, 0, n_c - 1)[0, :, 0, :]
    pick_c = (ci[..., None] == np.arange(n_c)).astype(np.float32)
    vis, pick_r = [], []
    for q_row0 in (0, half, grid_rows - NA_QROWS):
        k_row0 = int(np.clip(q_row0 - half, 0, grid_rows - NA_KROWS))
        qr = q_row0 + a
        kr = k_row0 + b
        rs = np.clip(qr - half, 0, grid_rows - NA_WIN_ROWS)
        vis.append(np.broadcast_to((kr >= rs) & (kr < rs + NA_WIN_ROWS) & (kc >= cs) & (kc < cs + NA_WIN_COLS), shape))
        ri = np.clip(kr - qr + NA_WIN_ROWS - 1, 0, n_r - 1)[:, 0, :, 0]
        pick_r.append((ri[..., None] == np.arange(n_r)).astype(np.float32))
    t = jnp.einsum('hrc,vabr,qkc->vhaqbk', rpb, jnp.asarray(np.stack(pick_r)), jnp.asarray(pick_c), precision=HI)
    t = jnp.where(jnp.asarray(np.stack(vis))[:, None], t, NEG)
    return t.reshape(3, N_HEADS, NA_QROWS * GRID_W, NA_KROWS * GRID_W)


DA_KEY_BLOCK = 512
DA_VROWS = 64 + 16


def _slab_reduce(x, op, slab=64):
    folded = op(x.reshape(x.shape[0] // slab, slab, x.shape[1]), axis=0)
    return op(folded, axis=0, keepdims=True)


def _da_kernel(*refs, lam_init, has_lat):
    if has_lat:
        q_ref, kl_ref, vtl_ref, kc_ref, vtc_ref, lam_ref, g_ref, o_ref, st_scr, p_scr = refs
    else:
        q_ref, kc_ref, vtc_ref, lam_ref, g_ref, o_ref = refs
    q = q_ref[...]
    tq = q.shape[0]
    lp = lam_ref[...]
    lam = (jnp.exp(jnp.sum(lp[0:1] * lp[1:2], axis=1, keepdims=True))
           - jnp.exp(jnp.sum(lp[2:3] * lp[3:4], axis=1, keepdims=True)) + lam_init)
    qmap = _lane_group(256, DIFF_DQK)
    zero = jnp.zeros_like(q)
    heads = range(N_HEADS)
    hs = [slice(h * 64, (h + 1) * 64) for h in heads]
    vrows = [slice(h * DA_VROWS, (h + 1) * DA_VROWS) for h in heads]
    qs = [jnp.concatenate([jnp.where(qmap == 2 * h, q, zero), jnp.where(qmap == 2 * h + 1, q, zero)], axis=0)
          for h in heads]
    kc = kc_ref[...]
    st = [_dot_nt(kc, qs[h]) for h in heads]
    m = [_slab_reduce(st[h], jnp.max) for h in heads]
    acc = [_dot(vtc_ref[vrows[h], :], jnp.exp2((st[h] - m[h]).astype(BF16))) for h in heads]
    if has_lat:
        n_blocks = kl_ref.shape[0] // DA_KEY_BLOCK

        def block(i):
            start = i * DA_KEY_BLOCK
            return pl.ds(start if isinstance(i, int) else pl.multiple_of(start, DA_KEY_BLOCK), DA_KEY_BLOCK)

        def scores(i, slot):
            kb = kl_ref[block(i), :]
            for h in heads:
                st_scr[slot, h] = _dot_nt(kb, qs[h])

        def softmax_stage(slot, m):
            m_new = [jnp.maximum(m[h], _slab_reduce(st_scr[slot, h], jnp.max)) for h in heads]
            alpha = [jnp.exp2(m[h] - m_new[h]) for h in heads]
            for h in heads:
                p_scr[slot, h] = jnp.exp2((st_scr[slot, h] - m_new[h]).astype(BF16))
            return m_new, alpha

        def value_stage(acc, alpha, slot, i):
            return [alpha[h] * acc[h] + _dot(vtl_ref[vrows[h], block(i)], p_scr[slot, h]) for h in heads]

        scores(0, 0)
        alpha = None
        for i in range(n_blocks):
            slot = i % 2
            if i + 1 < n_blocks:
                scores(i + 1, 1 - slot)
            m, alpha_i = softmax_stage(slot, m)
            if i > 0:
                acc = value_stage(acc, alpha, 1 - slot, i - 1)
            alpha = alpha_i
        acc = value_stage(acc, alpha, (n_blocks - 1) % 2, n_blocks - 1)
    outs = []
    for h in heads:
        den = acc[h][64:65, :]
        o_h = acc[h][0:64, 0:tq] * (1.0 / den[:, 0:tq]) - acc[h][0:64, tq:2 * tq] * (lam / den[:, tq:2 * tq])
        y_h = o_h * lax.rsqrt(jnp.mean(o_h * o_h, axis=0, keepdims=True) + EPS) * g_ref[hs[h], :]
        outs.append(y_h * (1.0 - lam_init))
    o_ref[...] = jnp.concatenate(outs, axis=0).T.astype(BF16)


def _da_attention(qkv_q, qkv_l, vt_l, qkv_c, vt_c, lam_params, gain_col, lam_init, tq):
    nb, nq = qkv_q.shape[0], qkv_q.shape[1]
    has_lat = qkv_l is not None
    full = lambda arr, c: pl.BlockSpec((None, arr.shape[1], 256), lambda b, j: (b, 0, c))
    full_t = lambda arr: pl.BlockSpec((None, arr.shape[1], arr.shape[2]), lambda b, j: (b, 0, 0))
    in_specs = [pl.BlockSpec((None, tq, 256), lambda b, j: (b, j, 0))]
    args = [qkv_q]
    if has_lat:
        in_specs += [full(qkv_l, 1), full_t(vt_l)]
        args += [qkv_l, vt_l]
    in_specs += [full(qkv_c, 1), full_t(vt_c),
                 pl.BlockSpec((4, DIFF_DQK), lambda b, j: (0, 0)), pl.BlockSpec((256, 1), lambda b, j: (0, 0))]
    args += [qkv_c, vt_c, lam_params, gain_col]
    scratch = []
    if has_lat:
        assert qkv_l.shape[1] % DA_KEY_BLOCK == 0
        scratch = [pltpu.VMEM((2, N_HEADS, DA_KEY_BLOCK, 2 * tq), F32),
                   pltpu.VMEM((2, N_HEADS, DA_KEY_BLOCK, 2 * tq), BF16)]
    return pl.pallas_call(
        functools.partial(_da_kernel, lam_init=lam_init, has_lat=has_lat),
        out_shape=jax.ShapeDtypeStruct((nb, nq, 256), BF16),
        grid=(nb, nq // tq),
        in_specs=in_specs,
        out_specs=pl.BlockSpec((None, tq, 256), lambda b, j: (b, j, 0)),
        scratch_shapes=scratch,
        compiler_params=_params(("parallel", "arbitrary")),
        name="diff_attention",
    )(*args)


def _merge_kernel(x_ref, g1_ref, dnf_ref, dnb_ref, z_ref, glf_ref, glb_ref, r_ref, na_ref, df_ref, gate_ref,
                  bgate_ref, dng_ref, glg_ref, wb_ref, wo_ref, o_ref):
    def head_norm(o, gate, g):
        return ((o * lax.rsqrt(_group_mean(o * o, 64) + EPS) * g) * _silu(gate.astype(F32))).astype(BF16)

    ys = (head_norm(dnf_ref[...] + dnb_ref[...], z_ref[...], dng_ref[...]), na_ref[...],
          head_norm(glf_ref[...] + glb_ref[...], r_ref[...], glg_ref[...]), df_ref[...])
    acc = jnp.zeros(x_ref.shape, F32)
    for g in range(N_BRANCH):
        gate = _sigmoid(gate_ref[:, g * D_MODEL:(g + 1) * D_MODEL].astype(F32) + bgate_ref[g:g + 1, :])
        acc = acc + gate * _dot(ys[g], wb_ref[g])
    o_ref[...] = x_ref[...] + g1_ref[...] * _dot(acc.astype(BF16), wo_ref[...])


def _merge(x2, g1, dn_f, dn_b, gl_f, gl_b, y_na, y_df, p, b_gate, dn_g, gl_g, w_branch, w_out, rows_per_mod, tm):
    rows = x2.shape[0]
    row = lambda w, c=0: pl.BlockSpec((tm, w), lambda i: (i, c))
    const = lambda shape: pl.BlockSpec(shape, lambda i: (0,) * len(shape))
    return pl.pallas_call(
        _merge_kernel,
        out_shape=jax.ShapeDtypeStruct((rows, D_MODEL), F32),
        grid=(rows // tm,),
        in_specs=[row(D_MODEL),
                  pl.BlockSpec((None, 1, D_MODEL), lambda i: ((i * tm) // rows_per_mod, 0, 0)),
                  row(256), row(256), row(256, C_DNZ // 256),
                  row(256), row(256), row(256, C_GR // 256),
                  row(256), row(256), row(N_BRANCH * D_MODEL, C_GATE // (N_BRANCH * D_MODEL)),
                  const((N_BRANCH, D_MODEL)), const((1, 256)), const((1, 256)),
                  const((N_BRANCH, BRANCH_W, D_MODEL)), const((D_MODEL, D_MODEL))],
        out_specs=row(D_MODEL),
        compiler_params=_params(("parallel",)),
        name="merge",
    )(x2, g1, dn_f, dn_b, p, gl_f, gl_b, p, y_na, y_df, p, b_gate, dn_g, gl_g, w_branch, w_out)


FFN_COLS = 256


def _ffn_kernel(x_ref, xp_ref, xn_ref, gain_ref, sc_ref, sh_ref, g2_ref, wu_ref, wv_ref, cw_ref, cb_ref, wo_ref,
                o_ref, u_scr, act_scr, *, tm, seq_len):
    i = pl.program_id(0)
    first = (i * tm) % seq_len == 0
    last = ((i + 1) * tm) % seq_len == 0

    def norm(x):
        y = x * lax.rsqrt(jnp.mean(x * x, axis=-1, keepdims=True) + EPS)
        return (y * gain_ref[...]) * sc_ref[...] + sh_ref[...]

    x = x_ref[...]
    h = norm(x)
    he = jnp.concatenate([norm(xp_ref[...]), h, norm(xn_ref[...])], axis=0).astype(BF16)
    h = h.astype(BF16)
    rows = _iota((tm + 2 * SUBLANES, 1), 0)
    keep = jnp.logical_not((first & (rows < SUBLANES)) | (last & (rows >= tm + SUBLANES)))
    u_scr[...] = jnp.where(keep, _dot(he, wu_ref[...]), 0.0)
    for c in range(D_FF // FFN_COLS):
        cols = slice(c * FFN_COLS, (c + 1) * FFN_COLS)
        conv = (u_scr[SUBLANES - 1:SUBLANES - 1 + tm, cols] * cw_ref[0:1, cols]
                + u_scr[SUBLANES:SUBLANES + tm, cols] * cw_ref[1:2, cols]
                + u_scr[SUBLANES + 1:SUBLANES + 1 + tm, cols] * cw_ref[2:3, cols]) + cb_ref[:, cols]
        act_scr[:, cols] = (_silu(conv) * _dot(h, wv_ref[:, cols])).astype(BF16)
    o_ref[...] = x + g2_ref[...] * _dot(act_scr[...], wo_ref[...])


def _ffn(x2, gain, scale1p, shift, g2, w_u, w_v, conv_w, conv_b, w_o, rows_per_mod, seq_len, tm):
    rows = x2.shape[0]
    nb8 = rows // SUBLANES
    t8 = tm // SUBLANES
    const = lambda shape: pl.BlockSpec(shape, lambda i: (0,) * len(shape))
    weight = lambda shape: pl.BlockSpec(shape, lambda i: (0,) * len(shape), pipeline_mode=pl.Buffered(1))
    mod = pl.BlockSpec((None, 1, D_MODEL), lambda i: ((i * tm) // rows_per_mod, 0, 0))
    return pl.pallas_call(
        functools.partial(_ffn_kernel, tm=tm, seq_len=seq_len),
        out_shape=jax.ShapeDtypeStruct((rows, D_MODEL), F32),
        grid=(rows // tm,),
        in_specs=[pl.BlockSpec((tm, D_MODEL), lambda i: (i, 0)),
                  pl.BlockSpec((SUBLANES, D_MODEL), lambda i: (jnp.maximum(i * t8 - 1, 0), 0)),
                  pl.BlockSpec((SUBLANES, D_MODEL), lambda i: (jnp.minimum((i + 1) * t8, nb8 - 1), 0)),
                  const((1, D_MODEL)), mod, mod, mod,
                  weight((D_MODEL, D_FF)), weight((D_MODEL, D_FF)), const((3, D_FF)), const((1, D_FF)),
                  weight((D_FF, D_MODEL))],
        out_specs=pl.BlockSpec((tm, D_MODEL), lambda i: (i, 0)),
        scratch_shapes=[pltpu.VMEM((tm + 2 * SUBLANES, D_FF), F32), pltpu.VMEM((tm, D_FF), BF16)],
        compiler_params=_params(("parallel",)),
        name="conv_ffn",
    )(x2, x2, x2, gain, scale1p, shift, g2, w_u, w_v, conv_w, conv_b, w_o)


def _inproj_columns():
    widths = (768, 256, 8, 8, 768, 128, 128, 256, 256, 32, 768, 4096)
    o = np.concatenate([[0], np.cumsum(widths)])
    src = np.zeros(P_WIDTH, np.int64)
    sign = np.zeros(P_WIDTH, np.float32)

    def put(dst, cols, sgn=None):
        src[dst:dst + len(cols)] = cols
        sign[dst:dst + len(cols)] = 1.0 if sgn is None else sgn

    put(C_DN, np.arange(o[0], o[1]))
    put(C_NA, np.arange(o[4], o[5]))
    put(C_DF, np.arange(o[10], o[11]))
    put(C_DNZ, np.arange(o[1], o[2]))
    quarter = DIFF_DQK // 4
    perm = np.concatenate([np.arange(quarter, 2 * quarter), np.arange(0, quarter),
                           np.arange(3 * quarter, 4 * quarter), np.arange(2 * quarter, 3 * quarter)])
    sgn = np.concatenate([-np.ones(quarter), np.ones(quarter), -np.ones(quarter), np.ones(quarter)])
    j = np.arange(512)
    put(C_DFROT, o[10] + (j // DIFF_DQK) * DIFF_DQK + perm[j % DIFF_DQK], sgn[j % DIFF_DQK])
    put(C_GQK, np.arange(o[5], o[7]))
    put(C_GV, np.arange(o[7], o[8]))
    put(C_GR, np.arange(o[8], o[9]))
    put(C_MISC, np.concatenate([np.arange(o[2], o[4]), np.arange(o[9], o[10])]))
    put(C_GATE, np.arange(o[11], o[12]))
    return src, sign, perm


def _rope_tables(length):
    t = jnp.arange(length)
    row = (t // GRID_W).astype(F32)
    col = (t % GRID_W).astype(F32)
    n_freq = DIFF_DQK // 4
    inv_freq = jnp.power(jnp.float32(ROPE_THETA), -jnp.arange(n_freq, dtype=F32) / n_freq)
    ang_r = row[:, None] * inv_freq
    ang_c = col[:, None] * inv_freq
    ang = jnp.concatenate([ang_r, ang_r, ang_c, ang_c], axis=-1)
    reps = 256 // DIFF_DQK
    return jnp.tile(jnp.cos(ang), (1, reps)), jnp.tile(jnp.sin(ang), (1, reps))


def _values_transposed(v):
    nb, t, _ = v.shape
    v4 = v.reshape(nb, t, N_HEADS, 64)
    ones = jnp.ones((nb, t, N_HEADS, DA_VROWS - 64), v.dtype)
    return jnp.swapaxes(jnp.concatenate([v4, ones], axis=-1).reshape(nb, t, N_HEADS * DA_VROWS), 1, 2)


def _pad_lanes(v, offset, width):
    return jnp.zeros((1, width), F32).at[0, offset:offset + v.shape[0]].set(v)


def kernel(x, c, ctx, c_ctx, w_mod, b_mod, norm1_g, norm2_g, w_in, b_gate, dn_conv, dn_a_log, dn_dt_bias, dn_norm_g,
           na_q_norm, na_k_norm, na_rpb, gla_w_a2, gla_b_a, gla_norm_g, df_q_norm, df_k_norm, df_lambda, df_norm_g,
           w_branch, w_out, ffn_w_in, ffn_conv_w, ffn_conv_b, ffn_w_out):
    nb, seq, _ = x.shape
    n_lat, n_ctx = nb * seq, nb * CTX_LEN
    tm_lat, tm_ctx = 512, 256

    cc = jnp.zeros((16, D_MODEL), F32).at[0:nb].set(c).at[nb].set(c_ctx)
    mod = _modulation(cc, w_mod, b_mod)

    src, sign, perm = _inproj_columns()
    w_in_p = (jnp.take(w_in, jnp.asarray(src), axis=2) * jnp.asarray(sign)).astype(BF16)
    w_branch_b = w_branch.astype(BF16)
    w_out_b = w_out.astype(BF16)
    w_u = ffn_w_in[:, :, 0:D_FF].astype(BF16)
    w_v = ffn_w_in[:, :, D_FF:2 * D_FF].astype(BF16)
    w_o = ffn_w_out.astype(BF16)
    cos, sin = _rope_tables(seq)

    x2 = x.reshape(n_lat, D_MODEL)
    xc2 = ctx.reshape(n_ctx, D_MODEL)
    for li in range(DEPTH):
        with_ctx = li < DEPTH - 1
        lam_init = 0.8 - 0.6 * math.exp(-0.3 * li)
        m = mod[li]
        part = lambda k: m[:, k * D_MODEL:(k + 1) * D_MODEL]
        lat = lambda v: v[0:nb].reshape(nb, 1, D_MODEL)
        cx = lambda v: v[nb:nb + 1].reshape(1, 1, D_MODEL)
        sh1, sc1, g1, sh2, sc2, g2 = (part(k) for k in range(6))
        n1 = norm1_g[li].reshape(1, D_MODEL)
        n2 = norm2_g[li].reshape(1, D_MODEL)

        p_l, misc_l = _inproj(x2, n1, lat(1.0 + sc1), lat(sh1), w_in_p[li], seq, 1024)
        p_c, misc_c = _inproj(xc2, n1, cx(1.0 + sc1), cx(sh1), w_in_p[li], n_ctx, 1024)

        tile4 = lambda v: jnp.tile(v, 256 // v.shape[0]).reshape(1, 256)
        vecs = (dn_conv[li], _pad_lanes(dn_a_log[li].reshape(-1), 2 * N_HEADS, LANES),
                _pad_lanes(dn_dt_bias[li].reshape(-1), 2 * N_HEADS, LANES),
                tile4(na_q_norm[li]), tile4(na_k_norm[li]), tile4(df_q_norm[li]), tile4(df_k_norm[li]),
                tile4(df_q_norm[li][perm]), tile4(df_k_norm[li][perm]))
        dn_l, bg_l, na_l, df_l = _prep(p_l, misc_l, cos, sin, vecs, tp=tm_lat, seq_len=seq, rope=True)
        dn_c, bg_c, na_c, df_c = _prep(p_c, misc_c, cos, sin, vecs, tp=tm_ctx, seq_len=CTX_LEN, rope=False)

        b3 = lambda a, t: a.reshape(nb, t, a.shape[-1])
        dn_cf, dn_cb, dn_lf, dn_lb = _dn_scan(b3(dn_c, CTX_LEN), b3(bg_c, CTX_LEN), b3(dn_l, seq), b3(bg_l, seq))

        w2 = jnp.zeros((LANES, 256), F32)
        a1_off = 4 * N_HEADS
        w2 = w2.at[a1_off:a1_off + GLA_RANK, 0:128].set(gla_w_a2[li, 0])
        w2 = w2.at[a1_off + GLA_RANK:a1_off + 2 * GLA_RANK, 128:256].set(gla_w_a2[li, 1])
        gl_cf, gl_cb, gl_lf, gl_lb = _gla_scan(b3(p_c, CTX_LEN), b3(misc_c, CTX_LEN), b3(p_l, seq), b3(misc_l, seq),
                                               w2, gla_b_a[li].reshape(1, 256))

        y_na = _na_attention(b3(na_l, seq), b3(na_c, CTX_LEN), _na_bias_tables(na_rpb[li], seq // GRID_W))
        vt_l = _values_transposed(b3(df_l, seq)[:, :, 512:768])
        vt_c = _values_transposed(b3(df_c, CTX_LEN)[:, :, 512:768])
        df_gain = tile4(df_norm_g[li]).reshape(256, 1)
        y_df = _da_attention(b3(df_l, seq), b3(df_l, seq), vt_l, b3(df_c, CTX_LEN), vt_c, df_lambda[li],
                             df_gain, lam_init, 128)

        flat = lambda a: a.reshape(-1, a.shape[-1])
        gains = (b_gate[li], tile4(dn_norm_g[li]), tile4(gla_norm_g[li]), w_branch_b[li], w_out_b[li])
        ffn_w = (w_u[li], w_v[li], ffn_conv_w[li], ffn_conv_b[li].reshape(1, D_FF), w_o[li])
        x2 = _merge(x2, lat(g1), flat(dn_lf), flat(dn_lb), flat(gl_lf), flat(gl_lb), flat(y_na), flat(y_df), p_l,
                    *gains, seq, tm_lat)
        x2 = _ffn(x2, n2, lat(1.0 + sc2), lat(sh2), lat(g2), *ffn_w, seq, seq, tm_lat)
        if with_ctx:
            yc_na = _na_ctx_attention(b3(na_c, CTX_LEN))
            yc_df = _da_attention(b3(df_c, CTX_LEN), None, None, b3(df_c, CTX_LEN), vt_c, df_lambda[li],
                                  df_gain, lam_init, 128)
            xc2 = _merge(xc2, cx(g1), flat(dn_cf), flat(dn_cb), flat(gl_cf), flat(gl_cb), flat(yc_na), flat(yc_df),
                         p_c, *gains, n_ctx, tm_ctx)
            xc2 = _ffn(xc2, n2, cx(1.0 + sc2), cx(sh2), cx(g2), *ffn_w, n_ctx, CTX_LEN, tm_ctx)
    return x2.reshape(nb, seq, D_MODEL)
```

```python
import functools
import math

import numpy as np
import jax
import jax.numpy as jnp
from jax import lax
from jax.experimental import pallas as pl
from jax.experimental.pallas import tpu as pltpu

F32 = jnp.float32
BF16 = jnp.bfloat16
HI = lax.Precision.HIGHEST

D_MODEL = 1024
DEPTH = 4
GRID_W = 64
CTX_LEN = 256
N_BRANCH = 4
BRANCH_W = 256
CHUNK = 64
N_HEADS = 4
DN_DK = 64
DN_CONV = 3
NA_DH = 64
NA_WIN_ROWS = 8
NA_WIN_COLS = 16
GLA_DK = 32
GLA_RANK = 16
GLA_TAU = 16.0
DIFF_DQK = 32
ROPE_THETA = 10000.0
D_FF = 2816
EPS = 1e-6
NEG = -1e30
LOG2E = 1.4426950408889634

LANES = 128
SUBLANES = 8
HALO = 16
VMEM_LIMIT = 56 * 1024 * 1024

C_DN = 0
C_NA = 768
C_DF = 1536
C_DNZ = 2304
C_DFROT = 2560
C_GQK = 3072
C_GV = 3328
C_GR = 3584
C_MISC = 3840
C_GATE = 4096
P_WIDTH = 8192

GROUP = 4 * CHUNK
NA_QROWS = 4
NA_KROWS = 12


def _dot(a, b, prec=None):
    return jnp.dot(a, b, preferred_element_type=F32, precision=prec)


def _dot_nt(a, b, prec=None):
    return lax.dot_general(a, b, (((1,), (1,)), ((), ())), preferred_element_type=F32, precision=prec)


def _dot_tn(a, b, prec=None):
    return lax.dot_general(a, b, (((0,), (0,)), ((), ())), preferred_element_type=F32, precision=prec)


def _iota(shape, dim):
    return lax.broadcasted_iota(jnp.int32, shape, dim)


def _group_mask(rows, cols, rgroup, cgroup):
    return (_iota((rows, cols), 0) // rgroup == _iota((rows, cols), 1) // cgroup).astype(F32)


def _group_mean(xsq, width):
    n = xsq.shape[-1]
    ones = _group_mask(n, n, width, width).astype(BF16)
    return _dot_mask(xsq, ones) * (1.0 / width)


def _split(x, terms):
    parts = []
    for _ in range(terms):
        piece = x.astype(BF16)
        parts.append(piece)
        x = x - piece.astype(F32)
    return parts


def _dot_mask(x, mask01, terms=2):
    m = mask01.astype(BF16)
    return sum(_dot(piece, m) for piece in _split(x, terms))


def _mask_dot(mask01, x, terms=3):
    m = mask01.astype(BF16)
    return sum(_dot(m, piece) for piece in _split(x, terms))


def _sigmoid(x):
    return 0.5 * jnp.tanh(0.5 * x) + 0.5


def _silu(x):
    return x * _sigmoid(x)


def _softplus(x):
    return jnp.maximum(x, 0.0) + jnp.log1p(jnp.exp(-jnp.abs(x)))


def _params(sem):
    return pltpu.CompilerParams(dimension_semantics=sem, vmem_limit_bytes=VMEM_LIMIT)


def _mod_kernel(c_ref, w_ref, b_ref, o_ref):
    o_ref[...] = _dot(_silu(c_ref[...]), w_ref[...], HI) + b_ref[...]


def _modulation(cc, w_mod, b_mod):
    tn = 1024
    return pl.pallas_call(
        _mod_kernel,
        out_shape=jax.ShapeDtypeStruct((DEPTH, 16, 6 * D_MODEL), F32),
        grid=(DEPTH, 6 * D_MODEL // tn),
        in_specs=[pl.BlockSpec((16, D_MODEL), lambda l, j: (0, 0)),
                  pl.BlockSpec((None, D_MODEL, tn), lambda l, j: (l, 0, j)),
                  pl.BlockSpec((None, 1, tn), lambda l, j: (l, 0, j))],
        out_specs=pl.BlockSpec((None, 16, tn), lambda l, j: (l, 0, j)),
        compiler_params=_params(("arbitrary", "arbitrary")),
        name="modulation",
    )(cc, w_mod, b_mod.reshape(DEPTH, 1, 6 * D_MODEL))


def _inproj_kernel(x_ref, g_ref, sc_ref, sh_ref, w_ref, o_ref, misc_ref, h_scr, *, tn):
    j = pl.program_id(1)

    @pl.when(j == 0)
    def _():
        x = x_ref[...]
        y = x * lax.rsqrt(jnp.mean(x * x, axis=-1, keepdims=True) + EPS)
        h_scr[...] = ((y * g_ref[...]) * sc_ref[...] + sh_ref[...]).astype(BF16)

    acc = _dot(h_scr[...], w_ref[...])
    o_ref[...] = acc.astype(BF16)

    @pl.when(j == C_MISC // tn)
    def _():
        misc_ref[...] = acc[:, C_MISC % tn:C_MISC % tn + LANES]


def _inproj(x2, gain, scale1p, shift, w, rows_per_mod, tm):
    rows = x2.shape[0]
    tn = 2048
    mod_spec = pl.BlockSpec((None, 1, D_MODEL), lambda i, j: ((i * tm) // rows_per_mod, 0, 0))
    return pl.pallas_call(
        functools.partial(_inproj_kernel, tn=tn),
        out_shape=(jax.ShapeDtypeStruct((rows, P_WIDTH), BF16), jax.ShapeDtypeStruct((rows, LANES), F32)),
        grid=(rows // tm, P_WIDTH // tn),
        in_specs=[pl.BlockSpec((tm, D_MODEL), lambda i, j: (i, 0)),
                  pl.BlockSpec((1, D_MODEL), lambda i, j: (0, 0)),
                  mod_spec, mod_spec,
                  pl.BlockSpec((D_MODEL, tn), lambda i, j: (0, j))],
        out_specs=(pl.BlockSpec((tm, tn), lambda i, j: (i, j)), pl.BlockSpec((tm, LANES), lambda i, j: (i, 0))),
        scratch_shapes=[pltpu.VMEM((tm, D_MODEL), BF16)],
        compiler_params=_params(("parallel", "arbitrary")),
        name="inproj",
    )(x2, gain, scale1p, shift, w)


def _prep_kernel(dn_ref, dnp_ref, dnn_ref, misc_ref, na_ref, df_ref, dfrot_ref, cos_ref, sin_ref,
                 convw_ref, alog_ref, dt_ref, naq_ref, nak_ref, dfq_ref, dfk_ref, dfqr_ref, dfkr_ref,
                 dn_o, bg_o, na_o, df_o, xe_scr, *, tp, seq_len, rope):
    i = pl.program_id(0)
    first = (i * tp) % seq_len == 0
    last = ((i + 1) * tp) % seq_len == 0
    halo_prev = dnp_ref[...].astype(F32)[HALO - SUBLANES:HALO]
    halo_next = dnn_ref[...].astype(F32)[0:SUBLANES]
    xe_scr[0:SUBLANES, :] = jnp.where(first, 0.0, halo_prev)
    xe_scr[SUBLANES:SUBLANES + tp, :] = dn_ref[...].astype(F32)
    xe_scr[SUBLANES + tp:2 * SUBLANES + tp, :] = jnp.where(last, 0.0, halo_next)
    cw = convw_ref[...]
    conv = (xe_scr[SUBLANES - 1:SUBLANES - 1 + tp, :] * cw[0:1]
            + xe_scr[SUBLANES:SUBLANES + tp, :] * cw[1:2]
            + xe_scr[SUBLANES + 1:SUBLANES + 1 + tp, :] * cw[2:3])
    s = _silu(conv)
    q, k, v = s[:, 0:256], s[:, 256:512], s[:, 512:768]
    qn = q * lax.rsqrt(_group_mean(q * q, DN_DK) * DN_DK + EPS) * (DN_DK ** -0.5)
    kn = k * lax.rsqrt(_group_mean(k * k, DN_DK) * DN_DK + EPS)
    dn_o[:, 0:256] = qn
    dn_o[:, 256:512] = kn
    dn_o[:, 512:768] = v
    m = misc_ref[...]
    beta = _sigmoid(m)
    g = -jnp.exp(alog_ref[...]) * _softplus(m + dt_ref[...])
    bg_o[...] = jnp.where(_iota(m.shape, 1) < 2 * N_HEADS, beta, g)
    q, k = na_ref[:, 0:256].astype(F32), na_ref[:, 256:512].astype(F32)
    qn = q * lax.rsqrt(_group_mean(q * q, NA_DH) + EPS) * naq_ref[...] * (NA_DH ** -0.5)
    kn = k * lax.rsqrt(_group_mean(k * k, NA_DH) + EPS) * nak_ref[...]
    na_o[:, 0:256] = qn.astype(BF16)
    na_o[:, 256:512] = kn.astype(BF16)
    na_o[:, 512:768] = na_ref[:, 512:768]
    q, k = df_ref[:, 0:256].astype(F32), df_ref[:, 256:512].astype(F32)
    rq = lax.rsqrt(_group_mean(q * q, DIFF_DQK) + EPS)
    rk = lax.rsqrt(_group_mean(k * k, DIFF_DQK) + EPS)
    qn = q * rq * dfq_ref[...]
    kn = k * rk * dfk_ref[...]
    if rope:
        xr = dfrot_ref[...].astype(F32)
        cos, sin = cos_ref[...], sin_ref[...]
        qn = qn * cos + (xr[:, 0:256] * rq * dfqr_ref[...]) * sin
        kn = kn * cos + (xr[:, 256:512] * rk * dfkr_ref[...]) * sin
    df_o[:, 0:256] = (qn * (DIFF_DQK ** -0.5 * LOG2E)).astype(BF16)
    df_o[:, 256:512] = kn.astype(BF16)
    df_o[:, 512:768] = df_ref[:, 512:768]


def _prep(p, misc, cos, sin, vecs, *, tp, seq_len, rope):
    rows = p.shape[0]
    nb8 = rows // HALO
    t8 = tp // HALO
    nseq_tiles = max(seq_len // tp, 1)
    row_vec = lambda w: pl.BlockSpec((1, w), lambda i: (0, 0))
    kern = functools.partial(_prep_kernel, tp=tp, seq_len=seq_len, rope=rope)
    return pl.pallas_call(
        kern,
        out_shape=(jax.ShapeDtypeStruct((rows, 768), F32), jax.ShapeDtypeStruct((rows, LANES), F32),
                   jax.ShapeDtypeStruct((rows, 768), BF16), jax.ShapeDtypeStruct((rows, 768), BF16)),
        grid=(rows // tp,),
        in_specs=[pl.BlockSpec((tp, 768), lambda i: (i, C_DN // 768)),
                  pl.BlockSpec((HALO, 768), lambda i: (jnp.maximum(i * t8 - 1, 0), C_DN // 768)),
                  pl.BlockSpec((HALO, 768), lambda i: (jnp.minimum((i + 1) * t8, nb8 - 1), C_DN // 768)),
                  pl.BlockSpec((tp, LANES), lambda i: (i, 0)),
                  pl.BlockSpec((tp, 768), lambda i: (i, C_NA // 768)),
                  pl.BlockSpec((tp, 768), lambda i: (i, C_DF // 768)),
                  pl.BlockSpec((tp, 512), lambda i: (i, C_DFROT // 512)),
                  pl.BlockSpec((tp, 256), lambda i: (i % nseq_tiles, 0)),
                  pl.BlockSpec((tp, 256), lambda i: (i % nseq_tiles, 0)),
                  pl.BlockSpec((DN_CONV, 768), lambda i: (0, 0)),
                  row_vec(LANES), row_vec(LANES),
                  row_vec(256), row_vec(256), row_vec(256), row_vec(256), row_vec(256), row_vec(256)],
        out_specs=(pl.BlockSpec((tp, 768), lambda i: (i, 0)), pl.BlockSpec((tp, LANES), lambda i: (i, 0)),
                   pl.BlockSpec((tp, 768), lambda i: (i, 0)), pl.BlockSpec((tp, 768), lambda i: (i, 0))),
        scratch_shapes=[pltpu.VMEM((tp + 2 * SUBLANES, 768), F32)],
        compiler_params=_params(("parallel",)),
        name="prep",
    )(p, p, p, misc, p, p, p, cos, sin, *vecs)


def _mm(a, b):
    return _dot(a.astype(BF16), b.astype(BF16))


def _mm_nt(a, b):
    return _dot_nt(a.astype(BF16), b.astype(BF16))


def _mm_tn(a, b):
    return _dot_tn(a.astype(BF16), b.astype(BF16))


def _tile4(x):
    return jnp.concatenate([x, x, x, x], axis=0)


def _tri_consts(d, width):
    rows = _iota((CHUNK, width), 0)
    lane_j = _iota((CHUNK, width), 1) % CHUNK
    incl = (lane_j <= rows) if d == 0 else (lane_j >= rows)
    strict = (lane_j < rows) if d == 0 else (lane_j > rows)
    eye_cat = (lane_j == rows).astype(F32)
    r64 = _iota((CHUNK, CHUNK), 0)
    c64 = _iota((CHUNK, CHUNK), 1)
    cum_mat = ((c64 <= r64) if d == 0 else (c64 >= r64)).astype(F32)
    return incl, strict, eye_cat, cum_mat


def _dn_group(qkv_f, bg_f, qkv_b, bg_b, of_ref, ob_ref, sf_scr, sb_scr):
    t = qkv_f.shape[0]
    n = t // CHUNK
    rows = _iota((CHUNK, 256), 0)
    lane_j = _iota((CHUNK, 256), 1) % CHUNK
    eye_cat = (lane_j == rows).astype(F32)
    mask_bd = _iota((256, 256), 0) // CHUNK == _iota((256, 256), 1) // CHUNK
    zero_b = jnp.zeros((), BF16)

    def bd(x):
        return jnp.where(mask_bd, _tile4(x.astype(BF16)), zero_b)

    work = []
    for d, (qkv, bg) in enumerate(((qkv_f, bg_f), (qkv_b, bg_b))):
        incl = (lane_j <= rows) if d == 0 else (lane_j >= rows)
        strict = (lane_j < rows) if d == 0 else (lane_j > rows)
        srow = _iota((LANES, 512), 0)
        scol = _iota((LANES, 512), 1)
        sel = (srow == (scol // 256) * 2 * N_HEADS + d * N_HEADS + (scol % 256) // CHUNK).astype(F32)
        e = _dot_mask(bg[...], sel, terms=3)
        rt = _iota((t, t), 0)
        ct = _iota((t, t), 1)
        cum = ((rt // CHUNK == ct // CHUNK) & ((ct <= rt) if d == 0 else (ct >= rt))).astype(F32)
        gc_all = _mask_dot(cum, e[:, 256:512])
        for c in (range(n) if d == 0 else range(n - 1, -1, -1)):
            sl = slice(c * CHUNK, (c + 1) * CHUNK)
            q, k, v = qkv[sl, 0:256], qkv[sl, 256:512], qkv[sl, 512:768]
            beta, gc = e[sl, 0:256], gc_all[sl]
            gc_row = jnp.sum(eye_cat * gc, axis=0, keepdims=True)
            decay = jnp.exp(jnp.where(incl, gc - gc_row, -jnp.inf))
            gc_last = gc[CHUNK - 1:CHUNK] if d == 0 else gc[0:1]
            kb = k * beta
            gram = _dot_nt(jnp.concatenate([kb, q], axis=0).astype(BF16), bd(k))
            n_cat = jnp.where(strict, gram[0:CHUNK] * decay, 0.0)
            egc = jnp.exp(gc)
            work.append(dict(
                d=d, sl=sl, n_cat=n_cat, a_cat=jnp.where(incl, gram[CHUNK:2 * CHUNK] * decay, 0.0).astype(BF16),
                rhs_u=bd(v * beta), rhs_w=bd(kb * egc), qd=(q * egc).astype(BF16),
                kd=(k * jnp.exp(gc_last - gc)).astype(BF16), s_decay=jnp.exp(gc_last),
                p=eye_cat - jnp.where(rows // 2 == lane_j // 2, n_cat, 0.0)))
    size = 2
    while size < CHUNK:
        off = (rows // (2 * size) == lane_j // (2 * size)) & (rows // size != lane_j // size)
        half = [_dot(w["p"].astype(BF16), bd(jnp.where(off, w["n_cat"], 0.0))) for w in work]
        for w, hx in zip(work, half):
            w["p"] = w["p"] - _dot(hx.astype(BF16), bd(w["p"]))
        size *= 2
    for w in work:
        pb = w["p"].astype(BF16)
        w["u"] = _dot(pb, w["rhs_u"])
        w["wq"] = jnp.concatenate([_dot(pb, w["rhs_w"]).astype(BF16), w["qd"]], axis=0)
    state = [sf_scr[...], sb_scr[...]]
    out = (of_ref, ob_ref)
    for c in range(n):
        for d in range(2):
            w = work[d * n + c]
            ws = _dot(w["wq"], state[d].astype(BF16))
            v_new = w["u"] - ws[0:CHUNK]
            out[d][w["sl"], :] = ws[CHUNK:2 * CHUNK] + _dot(w["a_cat"], bd(v_new))
            upd = _dot_tn(w["kd"], v_new.astype(BF16))
            state[d] = state[d] * w["s_decay"] + jnp.where(mask_bd, upd, 0.0)
    sf_scr[...] = state[0]
    sb_scr[...] = state[1]


def _dn_kernel(cq, cbg, lqf, lbgf, lqb, lbgb, ocf, ocb, olf, olb, sf_scr, sb_scr):
    step = pl.program_id(1)

    @pl.when(step == 0)
    def _():
        sf_scr[...] = jnp.zeros_like(sf_scr)
        sb_scr[...] = jnp.zeros_like(sb_scr)
        _dn_group(cq, cbg, cq, cbg, ocf, ocb, sf_scr, sb_scr)

    @pl.when(step > 0)
    def _():
        _dn_group(lqf, lbgf, lqb, lbgb, olf, olb, sf_scr, sb_scr)


def _scan_specs(width, ngroups):
    ctx = pl.BlockSpec((None, CTX_LEN, width), lambda b, s: (b, 0, 0))
    fwd = pl.BlockSpec((None, GROUP, width), lambda b, s: (b, jnp.maximum(s - 1, 0), 0))
    bwd = pl.BlockSpec((None, GROUP, width), lambda b, s: (b, ngroups - 1 - jnp.maximum(s - 1, 0), 0))
    return ctx, fwd, bwd


def _dn_scan(qkv_c, bg_c, qkv_l, bg_l):
    nb, seq = qkv_l.shape[0], qkv_l.shape[1]
    ng = seq // GROUP
    c768, f768, b768 = _scan_specs(768, ng)
    c128, f128, b128 = _scan_specs(LANES, ng)
    c256, f256, b256 = _scan_specs(256, ng)
    return pl.pallas_call(
        _dn_kernel,
        out_shape=(jax.ShapeDtypeStruct((nb, CTX_LEN, 256), F32), jax.ShapeDtypeStruct((nb, CTX_LEN, 256), F32),
                   jax.ShapeDtypeStruct((nb, seq, 256), F32), jax.ShapeDtypeStruct((nb, seq, 256), F32)),
        grid=(nb, ng + 1),
        in_specs=[c768, c128, f768, f128, b768, b128],
        out_specs=(c256, c256, f256, b256),
        scratch_shapes=[pltpu.VMEM((256, 256), F32), pltpu.VMEM((256, 256), F32)],
        compiler_params=_params(("parallel", "arbitrary")),
        name="dn_scan",
    )(qkv_c, bg_c, qkv_l, bg_l, qkv_l, bg_l)


def _gla_group(qk_f, v_f, m_f, qk_b, v_b, m_b, of_ref, ob_ref, sf_scr, sb_scr, w2, ba):
    t = qk_f.shape[0]
    n = t // CHUNK
    rows = _iota((CHUNK, 256), 0)
    lane_j = _iota((CHUNK, 256), 1) % CHUNK
    mask_k = _iota((256, 128), 0) // CHUNK == _iota((256, 128), 1) // GLA_DK
    mask_v = _iota((256, 256), 0) // CHUNK == _iota((256, 256), 1) // CHUNK
    mask_s = _iota((128, 256), 0) // GLA_DK == _iota((128, 256), 1) // CHUNK
    eye = (_iota((128, 128), 0) == _iota((128, 128), 1)).astype(F32)
    zero_b = jnp.zeros((), BF16)
    work = []
    for d, (qk, v, misc) in enumerate(((qk_f, v_f, m_f), (qk_b, v_b, m_b))):
        incl = (lane_j <= rows) if d == 0 else (lane_j >= rows)
        logit = _dot(misc[...], w2[:, d * 128:(d + 1) * 128], HI) + ba[:, d * 128:(d + 1) * 128]
        log_a = (jnp.minimum(logit, 0.0) - jnp.log1p(jnp.exp(-jnp.abs(logit)))) * (1.0 / GLA_TAU)
        rt = _iota((t, t), 0)
        ct = _iota((t, t), 1)
        cum = ((rt // CHUNK == ct // CHUNK) & ((ct <= rt) if d == 0 else (ct >= rt))).astype(F32)
        b_all = _mask_dot(cum, log_a)
        mid = CHUNK // 2 - 1 if d == 0 else CHUNK // 2
        last = CHUNK - 1 if d == 0 else 0
        for c in (range(n) if d == 0 else range(n - 1, -1, -1)):
            sl = slice(c * CHUNK, (c + 1) * CHUNK)
            q = qk[sl, 0:128].astype(F32) * (GLA_DK ** -0.5)
            k = qk[sl, 128:256].astype(F32)
            vb = v[sl, :]
            b = b_all[sl]
            b_mid = b[mid:mid + 1]
            b_last = b[last:last + 1]
            ke = jnp.where(mask_k, _tile4((k * jnp.exp(b_mid - b)).astype(BF16)), zero_b)
            a = jnp.where(incl, _dot_nt((q * jnp.exp(b - b_mid)).astype(BF16), ke), 0.0)
            o_intra = _dot(a.astype(BF16), jnp.where(mask_v, _tile4(vb), zero_b))
            ds = jnp.where(mask_s, _dot_tn((k * jnp.exp(b_last - b)).astype(BF16), vb), 0.0)
            dec_col = jnp.sum(eye * jnp.exp(b_last), axis=1, keepdims=True)
            work.append(dict(sl=sl, o_intra=o_intra, qb=(q * jnp.exp(b)).astype(BF16), ds=ds, dec_col=dec_col))
    state = [sf_scr[...], sb_scr[...]]
    out = (of_ref, ob_ref)
    for c in range(n):
        for d in range(2):
            w = work[d * n + c]
            out[d][w["sl"], :] = w["o_intra"] + _dot(w["qb"], state[d].astype(BF16))
            state[d] = w["dec_col"] * state[d] + w["ds"]
    sf_scr[...] = state[0]
    sb_scr[...] = state[1]


def _gla_kernel(cqk, cv, cm, lqkf, lvf, lmf, lqkb, lvb, lmb, w2_ref, ba_ref,
                ocf, ocb, olf, olb, sf_scr, sb_scr):
    step = pl.program_id(1)
    w2 = w2_ref[...]
    ba = ba_ref[...]

    @pl.when(step == 0)
    def _():
        sf_scr[...] = jnp.zeros_like(sf_scr)
        sb_scr[...] = jnp.zeros_like(sb_scr)
        _gla_group(cqk, cv, cm, cqk, cv, cm, ocf, ocb, sf_scr, sb_scr, w2, ba)

    @pl.when(step > 0)
    def _():
        _gla_group(lqkf, lvf, lmf, lqkb, lvb, lmb, olf, olb, sf_scr, sb_scr, w2, ba)


def _col_specs(width, col, ngroups):
    blk = col // width
    ctx = pl.BlockSpec((None, CTX_LEN, width), lambda b, s: (b, 0, blk))
    fwd = pl.BlockSpec((None, GROUP, width), lambda b, s: (b, jnp.maximum(s - 1, 0), blk))
    bwd = pl.BlockSpec((None, GROUP, width), lambda b, s: (b, ngroups - 1 - jnp.maximum(s - 1, 0), blk))
    return ctx, fwd, bwd


def _gla_scan(p_c, misc_c, p_l, misc_l, w2, ba):
    nb, seq = p_l.shape[0], p_l.shape[1]
    ng = seq // GROUP
    cqk, fqk, bqk = _col_specs(256, C_GQK, ng)
    cv, fv, bv = _col_specs(256, C_GV, ng)
    cm, fm, bm = _scan_specs(LANES, ng)
    c256, f256, b256 = _scan_specs(256, ng)
    const = lambda shape: pl.BlockSpec(shape, lambda b, s: (0, 0))
    return pl.pallas_call(
        _gla_kernel,
        out_shape=(jax.ShapeDtypeStruct((nb, CTX_LEN, 256), F32), jax.ShapeDtypeStruct((nb, CTX_LEN, 256), F32),
                   jax.ShapeDtypeStruct((nb, seq, 256), F32), jax.ShapeDtypeStruct((nb, seq, 256), F32)),
        grid=(nb, ng + 1),
        in_specs=[cqk, cv, cm, fqk, fv, fm, bqk, bv, bm, const((LANES, 256)), const((1, 256))],
        out_specs=(c256, c256, f256, b256),
        scratch_shapes=[pltpu.VMEM((128, 256), F32), pltpu.VMEM((128, 256), F32)],
        compiler_params=_params(("parallel", "arbitrary")),
        name="gla_scan",
    )(p_c, p_c, misc_c, p_l, p_l, misc_l, p_l, p_l, misc_l, w2, ba)


def _lane_group(width, group):
    return _iota((1, width), 1) // group


def _softmax_parts(parts):
    m = parts[0].max(axis=-1, keepdims=True)
    for s in parts[1:]:
        m = jnp.maximum(m, s.max(axis=-1, keepdims=True))
    ps = [jnp.exp(s - m) for s in parts]
    total = ps[0].sum(axis=-1, keepdims=True)
    for p in ps[1:]:
        total = total + p.sum(axis=-1, keepdims=True)
    return ps, 1.0 / total


def _na_kernel(q_ref, kl_ref, vl_ref, kc_ref, vc_ref, bias_ref, o_ref, *, grid_rows):
    blk = pl.program_id(1)
    key_row0 = jnp.clip(blk * NA_QROWS - NA_WIN_ROWS // 2, 0, grid_rows - NA_KROWS)
    start = pl.multiple_of(key_row0 * GRID_W, GRID_W)
    kw = kl_ref[pl.ds(start, NA_KROWS * GRID_W), :]
    vw = vl_ref[pl.ds(start, NA_KROWS * GRID_W), :]
    kc = kc_ref[...]
    vc = vc_ref[...]
    q = q_ref[...]
    head = _lane_group(256, NA_DH)
    acc = jnp.zeros(q.shape, F32)
    for h in range(N_HEADS):
        qh = jnp.where(head == h, q, jnp.zeros_like(q))
        (p_w, p_c), inv = _softmax_parts([_dot_nt(qh, kw) + bias_ref[h], _dot_nt(qh, kc)])
        o_h = (_dot(p_w.astype(BF16), vw) + _dot(p_c.astype(BF16), vc)) * inv
        acc = acc + jnp.where(head == h, o_h, 0.0)
    o_ref[...] = acc.astype(BF16)


def _na_attention(qkv_l, qkv_c, bias):
    nb, seq = qkv_l.shape[0], qkv_l.shape[1]
    tq = NA_QROWS * GRID_W
    nblk = seq // tq
    variant = lambda j: jnp.where(j == 0, 0, jnp.where(j == nblk - 1, 2, 1))
    kern = functools.partial(_na_kernel, grid_rows=seq // GRID_W)
    return pl.pallas_call(
        kern,
        out_shape=jax.ShapeDtypeStruct((nb, seq, 256), BF16),
        grid=(nb, nblk),
        in_specs=[pl.BlockSpec((None, tq, 256), lambda b, j: (b, j, 0)),
                  pl.BlockSpec((None, seq, 256), lambda b, j: (b, 0, 1)),
                  pl.BlockSpec((None, seq, 256), lambda b, j: (b, 0, 2)),
                  pl.BlockSpec((None, CTX_LEN, 256), lambda b, j: (b, 0, 1)),
                  pl.BlockSpec((None, CTX_LEN, 256), lambda b, j: (b, 0, 2)),
                  pl.BlockSpec((None, N_HEADS, tq, NA_KROWS * GRID_W), lambda b, j: (variant(j), 0, 0, 0))],
        out_specs=pl.BlockSpec((None, tq, 256), lambda b, j: (b, j, 0)),
        compiler_params=_params(("parallel", "arbitrary")),
        name="na_attention",
    )(qkv_l, qkv_l, qkv_l, qkv_c, qkv_c, bias)


def _na_ctx_kernel(q_ref, k_ref, v_ref, o_ref):
    q = q_ref[...]
    k = k_ref[...]
    v = v_ref[...]
    head = _lane_group(256, NA_DH)
    acc = jnp.zeros(q.shape, F32)
    for h in range(N_HEADS):
        qh = jnp.where(head == h, q, jnp.zeros_like(q))
        (p,), inv = _softmax_parts([_dot_nt(qh, k)])
        acc = acc + jnp.where(head == h, _dot(p.astype(BF16), v) * inv, 0.0)
    o_ref[...] = acc.astype(BF16)


def _na_ctx_attention(qkv_c):
    nb = qkv_c.shape[0]
    spec = lambda c: pl.BlockSpec((None, CTX_LEN, 256), lambda b: (b, 0, c))
    return pl.pallas_call(
        _na_ctx_kernel,
        out_shape=jax.ShapeDtypeStruct((nb, CTX_LEN, 256), BF16),
        grid=(nb,),
        in_specs=[spec(0), spec(1), spec(2)],
        out_specs=spec(0),
        compiler_params=_params(("parallel",)),
        name="na_ctx_attention",
    )(qkv_c, qkv_c, qkv_c)


def _na_bias_tables(rpb, grid_rows):
    a = np.arange(NA_QROWS)[:, None, None, None]
    qc = np.arange(GRID_W)[None, :, None, None]
    b = np.arange(NA_KROWS)[None, None, :, None]
    kc = np.arange(GRID_W)[None, None, None, :]
    shape = (NA_QROWS, GRID_W, NA_KROWS, GRID_W)
    half = NA_WIN_ROWS // 2
    n_r, n_c = 2 * NA_WIN_ROWS - 1, 2 * NA_WIN_COLS - 1
    cs = np.clip(qc - NA_WIN_COLS // 2, 0, GRID_W - NA_WIN_COLS)
    ci = np.clip(kc - qc + NA_WIN_COLS - 1, 0, n_c - 1)[0, :, 0, :]
    pick_c = (ci[..., None] == np.arange(n_c)).astype(np.float32)
    vis, pick_r = [], []
    for q_row0 in (0, half, grid_rows - NA_QROWS):
        k_row0 = int(np.clip(q_row0 - half, 0, grid_rows - NA_KROWS))
        qr = q_row0 + a
        kr = k_row0 + b
        rs = np.clip(qr - half, 0, grid_rows - NA_WIN_ROWS)
        vis.append(np.broadcast_to((kr >= rs) & (kr < rs + NA_WIN_ROWS) & (kc >= cs) & (kc < cs + NA_WIN_COLS), shape))
        ri = np.clip(kr - qr + NA_WIN_ROWS - 1, 0, n_r - 1)[:, 0, :, 0]
        pick_r.append((ri[..., None] == np.arange(n_r)).astype(np.float32))
    t = jnp.einsum('hrc,vabr,qkc->vhaqbk', rpb, jnp.asarray(np.stack(pick_r)), jnp.asarray(pick_c), precision=HI)
    t = jnp.where(jnp.asarray(np.stack(vis))[:, None], t, NEG)
    return t.reshape(3, N_HEADS, NA_QROWS * GRID_W, NA_KROWS * GRID_W)


DA_KEY_BLOCK = 512
DA_VROWS = 64 + 16


def _slab_reduce(x, op, slab=64):
    folded = op(x.reshape(x.shape[0] // slab, slab, x.shape[1]), axis=0)
    return op(folded, axis=0, keepdims=True)


def _da_kernel(*refs, lam_init, has_lat):
    if has_lat:
        q_ref, kl_ref, vtl_ref, kc_ref, vtc_ref, lam_ref, g_ref, o_ref, st_scr, p_scr = refs
    else:
        q_ref, kc_ref, vtc_ref, lam_ref, g_ref, o_ref = refs
    q = q_ref[...]
    tq = q.shape[0]
    lp = lam_ref[...]
    lam = (jnp.exp(jnp.sum(lp[0:1] * lp[1:2], axis=1, keepdims=True))
           - jnp.exp(jnp.sum(lp[2:3] * lp[3:4], axis=1, keepdims=True)) + lam_init)
    qmap = _lane_group(256, DIFF_DQK)
    zero = jnp.zeros_like(q)
    heads = range(N_HEADS)
    hs = [slice(h * 64, (h + 1) * 64) for h in heads]
    vrows = [slice(h * DA_VROWS, (h + 1) * DA_VROWS) for h in heads]
    qs = [jnp.concatenate([jnp.where(qmap == 2 * h, q, zero), jnp.where(qmap == 2 * h + 1, q, zero)], axis=0)
          for h in heads]
    kc = kc_ref[...]
    st = [_dot_nt(kc, qs[h]) for h in heads]
    m = [_slab_reduce(st[h], jnp.max) for h in heads]
    acc = [_dot(vtc_ref[vrows[h], :], jnp.exp2((st[h] - m[h]).astype(BF16))) for h in heads]
    if has_lat:
        n_blocks = kl_ref.shape[0] // DA_KEY_BLOCK

        def block(i):
            start = i * DA_KEY_BLOCK
            return pl.ds(start if isinstance(i, int) else pl.multiple_of(start, DA_KEY_BLOCK), DA_KEY_BLOCK)

        def scores(i, slot):
            kb = kl_ref[block(i), :]
            for h in heads:
                st_scr[slot, h] = _dot_nt(kb, qs[h])

        def softmax_stage(slot, m):
            m_new = [jnp.maximum(m[h], _slab_reduce(st_scr[slot, h], jnp.max)) for h in heads]
            alpha = [jnp.exp2(m[h] - m_new[h]) for h in heads]
            for h in heads:
                p_scr[slot, h] = jnp.exp2((st_scr[slot, h] - m_new[h]).astype(BF16))
            return m_new, alpha

        def value_stage(acc, alpha, slot, i):
            return [alpha[h] * acc[h] + _dot(vtl_ref[vrows[h], block(i)], p_scr[slot, h]) for h in heads]

        scores(0, 0)
        alpha = None
        for i in range(n_blocks):
            slot = i % 2
            if i + 1 < n_blocks:
                scores(i + 1, 1 - slot)
            m, alpha_i = softmax_stage(slot, m)
            if i > 0:
                acc = value_stage(acc, alpha, 1 - slot, i - 1)
            alpha = alpha_i
        acc = value_stage(acc, alpha, (n_blocks - 1) % 2, n_blocks - 1)
    outs = []
    for h in heads:
        den = acc[h][64:65, :]
        o_h = acc[h][0:64, 0:tq] * (1.0 / den[:, 0:tq]) - acc[h][0:64, tq:2 * tq] * (lam / den[:, tq:2 * tq])
        y_h = o_h * lax.rsqrt(jnp.mean(o_h * o_h, axis=0, keepdims=True) + EPS) * g_ref[hs[h], :]
        outs.append(y_h * (1.0 - lam_init))
    o_ref[...] = jnp.concatenate(outs, axis=0).T.astype(BF16)


def _da_attention(qkv_q, qkv_l, vt_l, qkv_c, vt_c, lam_params, gain_col, lam_init, tq):
    nb, nq = qkv_q.shape[0], qkv_q.shape[1]
    has_lat = qkv_l is not None
    full = lambda arr, c: pl.BlockSpec((None, arr.shape[1], 256), lambda b, j: (b, 0, c))
    full_t = lambda arr: pl.BlockSpec((None, arr.shape[1], arr.shape[2]), lambda b, j: (b, 0, 0))
    in_specs = [pl.BlockSpec((None, tq, 256), lambda b, j: (b, j, 0))]
    args = [qkv_q]
    if has_lat:
        in_specs += [full(qkv_l, 1), full_t(vt_l)]
        args += [qkv_l, vt_l]
    in_specs += [full(qkv_c, 1), full_t(vt_c),
                 pl.BlockSpec((4, DIFF_DQK), lambda b, j: (0, 0)), pl.BlockSpec((256, 1), lambda b, j: (0, 0))]
    args += [qkv_c, vt_c, lam_params, gain_col]
    scratch = []
    if has_lat:
        assert qkv_l.shape[1] % DA_KEY_BLOCK == 0
        scratch = [pltpu.VMEM((2, N_HEADS, DA_KEY_BLOCK, 2 * tq), F32),
                   pltpu.VMEM((2, N_HEADS, DA_KEY_BLOCK, 2 * tq), BF16)]
    return pl.pallas_call(
        functools.partial(_da_kernel, lam_init=lam_init, has_lat=has_lat),
        out_shape=jax.ShapeDtypeStruct((nb, nq, 256), BF16),
        grid=(nb, nq // tq),
        in_specs=in_specs,
        out_specs=pl.BlockSpec((None, tq, 256), lambda b, j: (b, j, 0)),
        scratch_shapes=scratch,
        compiler_params=_params(("parallel", "arbitrary")),
        name="diff_attention",
    )(*args)


def _merge_kernel(x_ref, g1_ref, dnf_ref, dnb_ref, z_ref, glf_ref, glb_ref, r_ref, na_ref, df_ref, gate_ref,
                  bgate_ref, dng_ref, glg_ref, wb_ref, wo_ref, o_ref):
    def head_norm(o, gate, g):
        return ((o * lax.rsqrt(_group_mean(o * o, 64) + EPS) * g) * _silu(gate.astype(F32))).astype(BF16)

    ys = (head_norm(dnf_ref[...] + dnb_ref[...], z_ref[...], dng_ref[...]), na_ref[...],
          head_norm(glf_ref[...] + glb_ref[...], r_ref[...], glg_ref[...]), df_ref[...])
    acc = jnp.zeros(x_ref.shape, F32)
    for g in range(N_BRANCH):
        gate = _sigmoid(gate_ref[:, g * D_MODEL:(g + 1) * D_MODEL].astype(F32) + bgate_ref[g:g + 1, :])
        acc = acc + gate * _dot(ys[g], wb_ref[g])
    o_ref[...] = x_ref[...] + g1_ref[...] * _dot(acc.astype(BF16), wo_ref[...])


def _merge(x2, g1, dn_f, dn_b, gl_f, gl_b, y_na, y_df, p, b_gate, dn_g, gl_g, w_branch, w_out, rows_per_mod, tm):
    rows = x2.shape[0]
    row = lambda w, c=0: pl.BlockSpec((tm, w), lambda i: (i, c))
    const = lambda shape: pl.BlockSpec(shape, lambda i: (0,) * len(shape))
    return pl.pallas_call(
        _merge_kernel,
        out_shape=jax.ShapeDtypeStruct((rows, D_MODEL), F32),
        grid=(rows // tm,),
        in_specs=[row(D_MODEL),
                  pl.BlockSpec((None, 1, D_MODEL), lambda i: ((i * tm) // rows_per_mod, 0, 0)),
                  row(256), row(256), row(256, C_DNZ // 256),
                  row(256), row(256), row(256, C_GR // 256),
                  row(256), row(256), row(N_BRANCH * D_MODEL, C_GATE // (N_BRANCH * D_MODEL)),
                  const((N_BRANCH, D_MODEL)), const((1, 256)), const((1, 256)),
                  const((N_BRANCH, BRANCH_W, D_MODEL)), const((D_MODEL, D_MODEL))],
        out_specs=row(D_MODEL),
        compiler_params=_params(("parallel",)),
        name="merge",
    )(x2, g1, dn_f, dn_b, p, gl_f, gl_b, p, y_na, y_df, p, b_gate, dn_g, gl_g, w_branch, w_out)


FFN_COLS = 256


def _ffn_kernel(x_ref, xp_ref, xn_ref, gain_ref, sc_ref, sh_ref, g2_ref, wu_ref, wv_ref, cw_ref, cb_ref, wo_ref,
                o_ref, u_scr, act_scr, *, tm, seq_len):
    i = pl.program_id(0)
    first = (i * tm) % seq_len == 0
    last = ((i + 1) * tm) % seq_len == 0

    def norm(x):
        y = x * lax.rsqrt(jnp.mean(x * x, axis=-1, keepdims=True) + EPS)
        return (y * gain_ref[...]) * sc_ref[...] + sh_ref[...]

    x = x_ref[...]
    h = norm(x)
    he = jnp.concatenate([norm(xp_ref[...]), h, norm(xn_ref[...])], axis=0).astype(BF16)
    h = h.astype(BF16)
    rows = _iota((tm + 2 * SUBLANES, 1), 0)
    keep = jnp.logical_not((first & (rows < SUBLANES)) | (last & (rows >= tm + SUBLANES)))
    u_scr[...] = jnp.where(keep, _dot(he, wu_ref[...]), 0.0)
    for c in range(D_FF // FFN_COLS):
        cols = slice(c * FFN_COLS, (c + 1) * FFN_COLS)
        conv = (u_scr[SUBLANES - 1:SUBLANES - 1 + tm, cols] * cw_ref[0:1, cols]
                + u_scr[SUBLANES:SUBLANES + tm, cols] * cw_ref[1:2, cols]
                + u_scr[SUBLANES + 1:SUBLANES + 1 + tm, cols] * cw_ref[2:3, cols]) + cb_ref[:, cols]
        act_scr[:, cols] = (_silu(conv) * _dot(h, wv_ref[:, cols])).astype(BF16)
    o_ref[...] = x + g2_ref[...] * _dot(act_scr[...], wo_ref[...])


def _ffn(x2, gain, scale1p, shift, g2, w_u, w_v, conv_w, conv_b, w_o, rows_per_mod, seq_len, tm):
    rows = x2.shape[0]
    nb8 = rows // SUBLANES
    t8 = tm // SUBLANES
    const = lambda shape: pl.BlockSpec(shape, lambda i: (0,) * len(shape))
    weight = lambda shape: pl.BlockSpec(shape, lambda i: (0,) * len(shape), pipeline_mode=pl.Buffered(1))
    mod = pl.BlockSpec((None, 1, D_MODEL), lambda i: ((i * tm) // rows_per_mod, 0, 0))
    return pl.pallas_call(
        functools.partial(_ffn_kernel, tm=tm, seq_len=seq_len),
        out_shape=jax.ShapeDtypeStruct((rows, D_MODEL), F32),
        grid=(rows // tm,),
        in_specs=[pl.BlockSpec((tm, D_MODEL), lambda i: (i, 0)),
                  pl.BlockSpec((SUBLANES, D_MODEL), lambda i: (jnp.maximum(i * t8 - 1, 0), 0)),
                  pl.BlockSpec((SUBLANES, D_MODEL), lambda i: (jnp.minimum((i + 1) * t8, nb8 - 1), 0)),
                  const((1, D_MODEL)), mod, mod, mod,
                  weight((D_MODEL, D_FF)), weight((D_MODEL, D_FF)), const((3, D_FF)), const((1, D_FF)),
                  weight((D_FF, D_MODEL))],
        out_specs=pl.BlockSpec((tm, D_MODEL), lambda i: (i, 0)),
        scratch_shapes=[pltpu.VMEM((tm + 2 * SUBLANES, D_FF), F32), pltpu.VMEM((tm, D_FF), BF16)],
        compiler_params=_params(("parallel",)),
        name="conv_ffn",
    )(x2, x2, x2, gain, scale1p, shift, g2, w_u, w_v, conv_w, conv_b, w_o)


def _inproj_columns():
    widths = (768, 256, 8, 8, 768, 128, 128, 256, 256, 32, 768, 4096)
    o = np.concatenate([[0], np.cumsum(widths)])
    src = np.zeros(P_WIDTH, np.int64)
    sign = np.zeros(P_WIDTH, np.float32)

    def put(dst, cols, sgn=None):
        src[dst:dst + len(cols)] = cols
        sign[dst:dst + len(cols)] = 1.0 if sgn is None else sgn

    put(C_DN, np.arange(o[0], o[1]))
    put(C_NA, np.arange(o[4], o[5]))
    put(C_DF, np.arange(o[10], o[11]))
    put(C_DNZ, np.arange(o[1], o[2]))
    quarter = DIFF_DQK // 4
    perm = np.concatenate([np.arange(quarter, 2 * quarter), np.arange(0, quarter),
                           np.arange(3 * quarter, 4 * quarter), np.arange(2 * quarter, 3 * quarter)])
    sgn = np.concatenate([-np.ones(quarter), np.ones(quarter), -np.ones(quarter), np.ones(quarter)])
    j = np.arange(512)
    put(C_DFROT, o[10] + (j // DIFF_DQK) * DIFF_DQK + perm[j % DIFF_DQK], sgn[j % DIFF_DQK])
    put(C_GQK, np.arange(o[5], o[7]))
    put(C_GV, np.arange(o[7], o[8]))
    put(C_GR, np.arange(o[8], o[9]))
    put(C_MISC, np.concatenate([np.arange(o[2], o[4]), np.arange(o[9], o[10])]))
    put(C_GATE, np.arange(o[11], o[12]))
    return src, sign, perm


def _inproj_weights(w_in):
    w = w_in.astype(BF16)
    widths = (768, 256, 8, 8, 768, 128, 128, 256, 256, 32, 768, 4096)
    o = np.concatenate([[0], np.cumsum(widths)])
    seg = lambda a, b: w[:, :, a:b]
    zeros = lambda n: jnp.zeros(w.shape[:2] + (n,), BF16)

    def rot(t):
        t6 = t.reshape(t.shape[:2] + (256 // DIFF_DQK, 2, 2, DIFF_DQK // 4))
        sign = jnp.asarray([-1.0, 1.0], BF16).reshape(1, 1, 1, 1, 2, 1)
        return (jnp.flip(t6, axis=4) * sign).reshape(t.shape)

    pieces = [seg(o[0], o[1]), seg(o[4], o[5]), seg(o[10], o[11]), seg(o[1], o[2]),
              rot(seg(o[10], o[10] + 256)), rot(seg(o[10] + 256, o[10] + 512)),
              seg(o[5], o[7]), seg(o[7], o[8]), seg(o[8], o[9]),
              seg(o[2], o[4]), seg(o[9], o[10]), zeros(C_GATE - C_MISC - 48), seg(o[11], o[12])]
    return jnp.concatenate(pieces, axis=-1)


def _rope_tables(length):
    t = jnp.arange(length)
    row = (t // GRID_W).astype(F32)
    col = (t % GRID_W).astype(F32)
    n_freq = DIFF_DQK // 4
    inv_freq = jnp.power(jnp.float32(ROPE_THETA), -jnp.arange(n_freq, dtype=F32) / n_freq)
    ang_r = row[:, None] * inv_freq
    ang_c = col[:, None] * inv_freq
    ang = jnp.concatenate([ang_r, ang_r, ang_c, ang_c], axis=-1)
    reps = 256 // DIFF_DQK
    return jnp.tile(jnp.cos(ang), (1, reps)), jnp.tile(jnp.sin(ang), (1, reps))


def _values_transposed(v):
    nb, t, _ = v.shape
    v4 = v.reshape(nb, t, N_HEADS, 64)
    ones = jnp.ones((nb, t, N_HEADS, DA_VROWS - 64), v.dtype)
    return jnp.swapaxes(jnp.concatenate([v4, ones], axis=-1).reshape(nb, t, N_HEADS * DA_VROWS), 1, 2)


def _pad_lanes(v, offset, width):
    return jnp.zeros((1, width), F32).at[0, offset:offset + v.shape[0]].set(v)


def kernel(x, c, ctx, c_ctx, w_mod, b_mod, norm1_g, norm2_g, w_in, b_gate, dn_conv, dn_a_log, dn_dt_bias, dn_norm_g,
           na_q_norm, na_k_norm, na_rpb, gla_w_a2, gla_b_a, gla_norm_g, df_q_norm, df_k_norm, df_lambda, df_norm_g,
           w_branch, w_out, ffn_w_in, ffn_conv_w, ffn_conv_b, ffn_w_out):
    nb, seq, _ = x.shape
    n_lat, n_ctx = nb * seq, nb * CTX_LEN
    tm_lat, tm_ctx = 512, 256

    cc = jnp.zeros((16, D_MODEL), F32).at[0:nb].set(c).at[nb].set(c_ctx)
    mod = _modulation(cc, w_mod, b_mod)

    src, sign, perm = _inproj_columns()
    w_in_p = _inproj_weights(w_in)
    w_branch_b = w_branch.astype(BF16)
    w_out_b = w_out.astype(BF16)
    w_u = ffn_w_in[:, :, 0:D_FF].astype(BF16)
    w_v = ffn_w_in[:, :, D_FF:2 * D_FF].astype(BF16)
    w_o = ffn_w_out.astype(BF16)
    cos, sin = _rope_tables(seq)

    x2 = x.reshape(n_lat, D_MODEL)
    xc2 = ctx.reshape(n_ctx, D_MODEL)
    for li in range(DEPTH):
        with_ctx = li < DEPTH - 1
        lam_init = 0.8 - 0.6 * math.exp(-0.3 * li)
        m = mod[li]
        part = lambda k: m[:, k * D_MODEL:(k + 1) * D_MODEL]
        lat = lambda v: v[0:nb].reshape(nb, 1, D_MODEL)
        cx = lambda v: v[nb:nb + 1].reshape(1, 1, D_MODEL)
        sh1, sc1, g1, sh2, sc2, g2 = (part(k) for k in range(6))
        n1 = norm1_g[li].reshape(1, D_MODEL)
        n2 = norm2_g[li].reshape(1, D_MODEL)

        p_l, misc_l = _inproj(x2, n1, lat(1.0 + sc1), lat(sh1), w_in_p[li], seq, 2048)
        p_c, misc_c = _inproj(xc2, n1, cx(1.0 + sc1), cx(sh1), w_in_p[li], n_ctx, 2048)

        tile4 = lambda v: jnp.tile(v, 256 // v.shape[0]).reshape(1, 256)
        vecs = (dn_conv[li], _pad_lanes(dn_a_log[li].reshape(-1), 2 * N_HEADS, LANES),
                _pad_lanes(dn_dt_bias[li].reshape(-1), 2 * N_HEADS, LANES),
                tile4(na_q_norm[li]), tile4(na_k_norm[li]), tile4(df_q_norm[li]), tile4(df_k_norm[li]),
                tile4(df_q_norm[li][perm]), tile4(df_k_norm[li][perm]))
        dn_l, bg_l, na_l, df_l = _prep(p_l, misc_l, cos, sin, vecs, tp=tm_lat, seq_len=seq, rope=True)
        dn_c, bg_c, na_c, df_c = _prep(p_c, misc_c, cos, sin, vecs, tp=tm_ctx, seq_len=CTX_LEN, rope=False)

        b3 = lambda a, t: a.reshape(nb, t, a.shape[-1])
        dn_cf, dn_cb, dn_lf, dn_lb = _dn_scan(b3(dn_c, CTX_LEN), b3(bg_c, CTX_LEN), b3(dn_l, seq), b3(bg_l, seq))

        w2 = jnp.zeros((LANES, 256), F32)
        a1_off = 4 * N_HEADS
        w2 = w2.at[a1_off:a1_off + GLA_RANK, 0:128].set(gla_w_a2[li, 0])
        w2 = w2.at[a1_off + GLA_RANK:a1_off + 2 * GLA_RANK, 128:256].set(gla_w_a2[li, 1])
        gl_cf, gl_cb, gl_lf, gl_lb = _gla_scan(b3(p_c, CTX_LEN), b3(misc_c, CTX_LEN), b3(p_l, seq), b3(misc_l, seq),
                                               w2, gla_b_a[li].reshape(1, 256))

        y_na = _na_attention(b3(na_l, seq), b3(na_c, CTX_LEN), _na_bias_tables(na_rpb[li], seq // GRID_W))
        vt_l = _values_transposed(b3(df_l, seq)[:, :, 512:768])
        vt_c = _values_transposed(b3(df_c, CTX_LEN)[:, :, 512:768])
        df_gain = tile4(df_norm_g[li]).reshape(256, 1)
        y_df = _da_attention(b3(df_l, seq), b3(df_l, seq), vt_l, b3(df_c, CTX_LEN), vt_c, df_lambda[li],
                             df_gain, lam_init, 256)

        flat = lambda a: a.reshape(-1, a.shape[-1])
        gains = (b_gate[li], tile4(dn_norm_g[li]), tile4(gla_norm_g[li]), w_branch_b[li], w_out_b[li])
        ffn_w = (w_u[li], w_v[li], ffn_conv_w[li], ffn_conv_b[li].reshape(1, D_FF), w_o[li])
        x2 = _merge(x2, lat(g1), flat(dn_lf), flat(dn_lb), flat(gl_lf), flat(gl_lb), flat(y_na), flat(y_df), p_l,
                    *gains, seq, tm_lat)
        x2 = _ffn(x2, n2, lat(1.0 + sc2), lat(sh2), lat(g2), *ffn_w, seq, seq, tm_lat)
        if with_ctx:
            yc_na = _na_ctx_attention(b3(na_c, CTX_LEN))
            yc_df = _da_attention(b3(df_c, CTX_LEN), None, None, b3(df_c, CTX_LEN), vt_c, df_lambda[li],
                                  df_gain, lam_init, 128)
            xc2 = _merge(xc2, cx(g1), flat(dn_cf), flat(dn_cb), flat(gl_cf), flat(gl_cb), flat(yc_na), flat(yc_df),
                         p_c, *gains, n_ctx, tm_ctx)
            xc2 = _ffn(xc2, n2, cx(1.0 + sc2), cx(sh2), cx(g2), *ffn_w, n_ctx, CTX_LEN, tm_ctx)
    return x2.reshape(nb, seq, D_MODEL)
```

```python
import functools
import math

import numpy as np
import jax
import jax.numpy as jnp
from jax import lax
from jax.experimental import pallas as pl
from jax.experimental.pallas import tpu as pltpu

F32 = jnp.float32
BF16 = jnp.bfloat16
HI = lax.Precision.HIGHEST

D_MODEL = 1024
DEPTH = 4
GRID_W = 64
CTX_LEN = 256
N_BRANCH = 4
BRANCH_W = 256
CHUNK = 64
N_HEADS = 4
DN_DK = 64
DN_CONV = 3
NA_DH = 64
NA_WIN_ROWS = 8
NA_WIN_COLS = 16
GLA_DK = 32
GLA_RANK = 16
GLA_TAU = 16.0
DIFF_DQK = 32
ROPE_THETA = 10000.0
D_FF = 2816
EPS = 1e-6
NEG = -1e30
LOG2E = 1.4426950408889634

LANES = 128
SUBLANES = 8
HALO = 16
VMEM_LIMIT = 56 * 1024 * 1024

C_DN = 0
C_NA = 768
C_DF = 1536
C_DNZ = 2304
C_DFROT = 2560
C_GQK = 3072
C_GV = 3328
C_GR = 3584
C_MISC = 3840
C_GATE = 4096
P_WIDTH = 8192

GROUP = 4 * CHUNK
NA_QROWS = 4
NA_KROWS = 12


def _dot(a, b, prec=None):
    return jnp.dot(a, b, preferred_element_type=F32, precision=prec)


def _dot_nt(a, b, prec=None):
    return lax.dot_general(a, b, (((1,), (1,)), ((), ())), preferred_element_type=F32, precision=prec)


def _dot_tn(a, b, prec=None):
    return lax.dot_general(a, b, (((0,), (0,)), ((), ())), preferred_element_type=F32, precision=prec)


def _iota(shape, dim):
    return lax.broadcasted_iota(jnp.int32, shape, dim)


def _group_mask(rows, cols, rgroup, cgroup):
    return (_iota((rows, cols), 0) // rgroup == _iota((rows, cols), 1) // cgroup).astype(F32)


def _group_mean(xsq, width):
    n = xsq.shape[-1]
    ones = _group_mask(n, n, width, width).astype(BF16)
    return _dot_mask(xsq, ones) * (1.0 / width)


def _split(x, terms):
    parts = []
    for _ in range(terms):
        piece = x.astype(BF16)
        parts.append(piece)
        x = x - piece.astype(F32)
    return parts


def _dot_mask(x, mask01, terms=2):
    m = mask01.astype(BF16)
    return sum(_dot(piece, m) for piece in _split(x, terms))


def _mask_dot(mask01, x, terms=3):
    m = mask01.astype(BF16)
    return sum(_dot(m, piece) for piece in _split(x, terms))


def _sigmoid(x):
    return 0.5 * jnp.tanh(0.5 * x) + 0.5


def _silu(x):
    return x * _sigmoid(x)


def _softplus(x):
    return jnp.maximum(x, 0.0) + jnp.log1p(jnp.exp(-jnp.abs(x)))


def _params(sem):
    return pltpu.CompilerParams(dimension_semantics=sem, vmem_limit_bytes=VMEM_LIMIT)


def _mod_kernel(c_ref, w_ref, b_ref, o_ref):
    o_ref[...] = _dot(_silu(c_ref[...]), w_ref[...], HI) + b_ref[...]


def _modulation(cc, w_mod, b_mod):
    tn = 1024
    return pl.pallas_call(
        _mod_kernel,
        out_shape=jax.ShapeDtypeStruct((DEPTH, 16, 6 * D_MODEL), F32),
        grid=(DEPTH, 6 * D_MODEL // tn),
        in_specs=[pl.BlockSpec((16, D_MODEL), lambda l, j: (0, 0)),
                  pl.BlockSpec((None, D_MODEL, tn), lambda l, j: (l, 0, j)),
                  pl.BlockSpec((None, 1, tn), lambda l, j: (l, 0, j))],
        out_specs=pl.BlockSpec((None, 16, tn), lambda l, j: (l, 0, j)),
        compiler_params=_params(("arbitrary", "arbitrary")),
        name="modulation",
    )(cc, w_mod, b_mod.reshape(DEPTH, 1, 6 * D_MODEL))


def _inproj_kernel(x_ref, g_ref, sc_ref, sh_ref, w_ref, o_ref, misc_ref, h_scr, *, tn):
    j = pl.program_id(1)

    @pl.when(j == 0)
    def _():
        x = x_ref[...]
        y = x * lax.rsqrt(jnp.mean(x * x, axis=-1, keepdims=True) + EPS)
        h_scr[...] = ((y * g_ref[...]) * sc_ref[...] + sh_ref[...]).astype(BF16)

    acc = _dot(h_scr[...], w_ref[...])
    o_ref[...] = acc.astype(BF16)

    @pl.when(j == C_MISC // tn)
    def _():
        misc_ref[...] = acc[:, C_MISC % tn:C_MISC % tn + LANES]


def _inproj(x2, gain, scale1p, shift, w, rows_per_mod, tm):
    rows = x2.shape[0]
    tn = 2048
    mod_spec = pl.BlockSpec((None, 1, D_MODEL), lambda i, j: ((i * tm) // rows_per_mod, 0, 0))
    return pl.pallas_call(
        functools.partial(_inproj_kernel, tn=tn),
        out_shape=(jax.ShapeDtypeStruct((rows, P_WIDTH), BF16), jax.ShapeDtypeStruct((rows, LANES), F32)),
        grid=(rows // tm, P_WIDTH // tn),
        in_specs=[pl.BlockSpec((tm, D_MODEL), lambda i, j: (i, 0)),
                  pl.BlockSpec((1, D_MODEL), lambda i, j: (0, 0)),
                  mod_spec, mod_spec,
                  pl.BlockSpec((D_MODEL, tn), lambda i, j: (0, j))],
        out_specs=(pl.BlockSpec((tm, tn), lambda i, j: (i, j)), pl.BlockSpec((tm, LANES), lambda i, j: (i, 0))),
        scratch_shapes=[pltpu.VMEM((tm, D_MODEL), BF16)],
        compiler_params=_params(("parallel", "arbitrary")),
        name="inproj",
    )(x2, gain, scale1p, shift, w)


def _prep_kernel(dn_ref, dnp_ref, dnn_ref, misc_ref, na_ref, df_ref, dfrot_ref, cos_ref, sin_ref,
                 convw_ref, alog_ref, dt_ref, naq_ref, nak_ref, dfq_ref, dfk_ref, dfqr_ref, dfkr_ref,
                 dn_o, bg_o, na_o, df_o, vt_o, xe_scr, *, tp, seq_len, rope):
    i = pl.program_id(0)
    first = (i * tp) % seq_len == 0
    last = ((i + 1) * tp) % seq_len == 0
    halo_prev = dnp_ref[...].astype(F32)[HALO - SUBLANES:HALO]
    halo_next = dnn_ref[...].astype(F32)[0:SUBLANES]
    xe_scr[0:SUBLANES, :] = jnp.where(first, 0.0, halo_prev)
    xe_scr[SUBLANES:SUBLANES + tp, :] = dn_ref[...].astype(F32)
    xe_scr[SUBLANES + tp:2 * SUBLANES + tp, :] = jnp.where(last, 0.0, halo_next)
    cw = convw_ref[...]
    conv = (xe_scr[SUBLANES - 1:SUBLANES - 1 + tp, :] * cw[0:1]
            + xe_scr[SUBLANES:SUBLANES + tp, :] * cw[1:2]
            + xe_scr[SUBLANES + 1:SUBLANES + 1 + tp, :] * cw[2:3])
    s = _silu(conv)
    q, k, v = s[:, 0:256], s[:, 256:512], s[:, 512:768]
    qn = q * lax.rsqrt(_group_mean(q * q, DN_DK) * DN_DK + EPS) * (DN_DK ** -0.5)
    kn = k * lax.rsqrt(_group_mean(k * k, DN_DK) * DN_DK + EPS)
    dn_o[:, 0:256] = qn
    dn_o[:, 256:512] = kn
    dn_o[:, 512:768] = v
    m = misc_ref[...]
    beta = _sigmoid(m)
    g = -jnp.exp(alog_ref[...]) * _softplus(m + dt_ref[...])
    bg_o[...] = jnp.where(_iota(m.shape, 1) < 2 * N_HEADS, beta, g)
    q, k = na_ref[:, 0:256].astype(F32), na_ref[:, 256:512].astype(F32)
    qn = q * lax.rsqrt(_group_mean(q * q, NA_DH) + EPS) * naq_ref[...] * (NA_DH ** -0.5)
    kn = k * lax.rsqrt(_group_mean(k * k, NA_DH) + EPS) * nak_ref[...]
    na_o[:, 0:256] = qn.astype(BF16)
    na_o[:, 256:512] = kn.astype(BF16)
    na_o[:, 512:768] = na_ref[:, 512:768]
    q, k = df_ref[:, 0:256].astype(F32), df_ref[:, 256:512].astype(F32)
    rq = lax.rsqrt(_group_mean(q * q, DIFF_DQK) + EPS)
    rk = lax.rsqrt(_group_mean(k * k, DIFF_DQK) + EPS)
    qn = q * rq * dfq_ref[...]
    kn = k * rk * dfk_ref[...]
    if rope:
        xr = dfrot_ref[...].astype(F32)
        cos, sin = cos_ref[...], sin_ref[...]
        qn = qn * cos + (xr[:, 0:256] * rq * dfqr_ref[...]) * sin
        kn = kn * cos + (xr[:, 256:512] * rk * dfkr_ref[...]) * sin
    df_o[:, 0:256] = (qn * (DIFF_DQK ** -0.5 * LOG2E)).astype(BF16)
    df_o[:, 256:512] = kn.astype(BF16)
    df_o[:, 512:768] = df_ref[:, 512:768]
    vt = df_ref[:, 512:768].astype(F32).T
    ones = jnp.ones((DA_VROWS - 64, tp), F32)
    vt_o[...] = jnp.concatenate([piece for h in range(N_HEADS) for piece in (vt[h * 64:(h + 1) * 64], ones)],
                                axis=0).astype(BF16)


def _prep(p, misc, cos, sin, vecs, *, tp, seq_len, rope):
    rows = p.shape[0]
    nb8 = rows // HALO
    t8 = tp // HALO
    nseq_tiles = max(seq_len // tp, 1)
    row_vec = lambda w: pl.BlockSpec((1, w), lambda i: (0, 0))
    kern = functools.partial(_prep_kernel, tp=tp, seq_len=seq_len, rope=rope)
    return pl.pallas_call(
        kern,
        out_shape=(jax.ShapeDtypeStruct((rows, 768), F32), jax.ShapeDtypeStruct((rows, LANES), F32),
                   jax.ShapeDtypeStruct((rows, 768), BF16), jax.ShapeDtypeStruct((rows, 768), BF16),
                   jax.ShapeDtypeStruct((rows // seq_len, N_HEADS * DA_VROWS, seq_len), BF16)),
        grid=(rows // tp,),
        in_specs=[pl.BlockSpec((tp, 768), lambda i: (i, C_DN // 768)),
                  pl.BlockSpec((HALO, 768), lambda i: (jnp.maximum(i * t8 - 1, 0), C_DN // 768)),
                  pl.BlockSpec((HALO, 768), lambda i: (jnp.minimum((i + 1) * t8, nb8 - 1), C_DN // 768)),
                  pl.BlockSpec((tp, LANES), lambda i: (i, 0)),
                  pl.BlockSpec((tp, 768), lambda i: (i, C_NA // 768)),
                  pl.BlockSpec((tp, 768), lambda i: (i, C_DF // 768)),
                  pl.BlockSpec((tp, 512), lambda i: (i, C_DFROT // 512)),
                  pl.BlockSpec((tp, 256), lambda i: (i % nseq_tiles, 0)),
                  pl.BlockSpec((tp, 256), lambda i: (i % nseq_tiles, 0)),
                  pl.BlockSpec((DN_CONV, 768), lambda i: (0, 0)),
                  row_vec(LANES), row_vec(LANES),
                  row_vec(256), row_vec(256), row_vec(256), row_vec(256), row_vec(256), row_vec(256)],
        out_specs=(pl.BlockSpec((tp, 768), lambda i: (i, 0)), pl.BlockSpec((tp, LANES), lambda i: (i, 0)),
                   pl.BlockSpec((tp, 768), lambda i: (i, 0)), pl.BlockSpec((tp, 768), lambda i: (i, 0)),
                   pl.BlockSpec((None, N_HEADS * DA_VROWS, tp), lambda i: (i // nseq_tiles, 0, i % nseq_tiles))),
        scratch_shapes=[pltpu.VMEM((tp + 2 * SUBLANES, 768), F32)],
        compiler_params=_params(("parallel",)),
        name="prep",
    )(p, p, p, misc, p, p, p, cos, sin, *vecs)


def _mm(a, b):
    return _dot(a.astype(BF16), b.astype(BF16))


def _mm_nt(a, b):
    return _dot_nt(a.astype(BF16), b.astype(BF16))


def _mm_tn(a, b):
    return _dot_tn(a.astype(BF16), b.astype(BF16))


def _tile4(x):
    return jnp.concatenate([x, x, x, x], axis=0)


def _tri_consts(d, width):
    rows = _iota((CHUNK, width), 0)
    lane_j = _iota((CHUNK, width), 1) % CHUNK
    incl = (lane_j <= rows) if d == 0 else (lane_j >= rows)
    strict = (lane_j < rows) if d == 0 else (lane_j > rows)
    eye_cat = (lane_j == rows).astype(F32)
    r64 = _iota((CHUNK, CHUNK), 0)
    c64 = _iota((CHUNK, CHUNK), 1)
    cum_mat = ((c64 <= r64) if d == 0 else (c64 >= r64)).astype(F32)
    return incl, strict, eye_cat, cum_mat


def _dn_group(qkv_f, bg_f, qkv_b, bg_b, of_ref, ob_ref, sf_scr, sb_scr):
    t = qkv_f.shape[0]
    n = t // CHUNK
    rows = _iota((CHUNK, 256), 0)
    lane_j = _iota((CHUNK, 256), 1) % CHUNK
    eye_cat = (lane_j == rows).astype(F32)
    mask_bd = _iota((256, 256), 0) // CHUNK == _iota((256, 256), 1) // CHUNK
    zero_b = jnp.zeros((), BF16)

    def bd(x):
        return jnp.where(mask_bd, _tile4(x.astype(BF16)), zero_b)

    work = []
    for d, (qkv, bg) in enumerate(((qkv_f, bg_f), (qkv_b, bg_b))):
        incl = (lane_j <= rows) if d == 0 else (lane_j >= rows)
        strict = (lane_j < rows) if d == 0 else (lane_j > rows)
        srow = _iota((LANES, 512), 0)
        scol = _iota((LANES, 512), 1)
        sel = (srow == (scol // 256) * 2 * N_HEADS + d * N_HEADS + (scol % 256) // CHUNK).astype(F32)
        e = _dot_mask(bg[...], sel, terms=3)
        rt = _iota((t, t), 0)
        ct = _iota((t, t), 1)
        cum = ((rt // CHUNK == ct // CHUNK) & ((ct <= rt) if d == 0 else (ct >= rt))).astype(F32)
        gc_all = _mask_dot(cum, e[:, 256:512])
        for c in (range(n) if d == 0 else range(n - 1, -1, -1)):
            sl = slice(c * CHUNK, (c + 1) * CHUNK)
            q, k, v = qkv[sl, 0:256], qkv[sl, 256:512], qkv[sl, 512:768]
            beta, gc = e[sl, 0:256], gc_all[sl]
            gc_row = jnp.sum(eye_cat * gc, axis=0, keepdims=True)
            decay = jnp.exp(jnp.where(incl, gc - gc_row, -jnp.inf))
            gc_last = gc[CHUNK - 1:CHUNK] if d == 0 else gc[0:1]
            kb = k * beta
            gram = _dot_nt(jnp.concatenate([kb, q], axis=0).astype(BF16), bd(k))
            n_cat = jnp.where(strict, gram[0:CHUNK] * decay, 0.0)
            egc = jnp.exp(gc)
            work.append(dict(
                d=d, sl=sl, n_cat=n_cat, a_cat=jnp.where(incl, gram[CHUNK:2 * CHUNK] * decay, 0.0).astype(BF16),
                rhs_u=bd(v * beta), rhs_w=bd(kb * egc), qd=(q * egc).astype(BF16),
                kd=(k * jnp.exp(gc_last - gc)).astype(BF16), s_decay=jnp.exp(gc_last),
                p=eye_cat - jnp.where(rows // 2 == lane_j // 2, n_cat, 0.0)))
    size = 2
    while size < CHUNK:
        off = (rows // (2 * size) == lane_j // (2 * size)) & (rows // size != lane_j // size)
        half = [_dot(w["p"].astype(BF16), bd(jnp.where(off, w["n_cat"], 0.0))) for w in work]
        for w, hx in zip(work, half):
            w["p"] = w["p"] - _dot(hx.astype(BF16), bd(w["p"]))
        size *= 2
    for w in work:
        pb = w["p"].astype(BF16)
        w["u"] = _dot(pb, w["rhs_u"])
        w["wq"] = jnp.concatenate([_dot(pb, w["rhs_w"]).astype(BF16), w["qd"]], axis=0)
    state = [sf_scr[...], sb_scr[...]]
    out = (of_ref, ob_ref)
    for c in range(n):
        for d in range(2):
            w = work[d * n + c]
            ws = _dot(w["wq"], state[d].astype(BF16))
            v_new = w["u"] - ws[0:CHUNK]
            out[d][w["sl"], :] = ws[CHUNK:2 * CHUNK] + _dot(w["a_cat"], bd(v_new))
            upd = _dot_tn(w["kd"], v_new.astype(BF16))
            state[d] = state[d] * w["s_decay"] + jnp.where(mask_bd, upd, 0.0)
    sf_scr[...] = state[0]
    sb_scr[...] = state[1]


def _dn_kernel(cq, cbg, lqf, lbgf, lqb, lbgb, ocf, ocb, olf, olb, sf_scr, sb_scr):
    step = pl.program_id(1)

    @pl.when(step == 0)
    def _():
        sf_scr[...] = jnp.zeros_like(sf_scr)
        sb_scr[...] = jnp.zeros_like(sb_scr)
        _dn_group(cq, cbg, cq, cbg, ocf, ocb, sf_scr, sb_scr)

    @pl.when(step > 0)
    def _():
        _dn_group(lqf, lbgf, lqb, lbgb, olf, olb, sf_scr, sb_scr)


def _scan_specs(width, ngroups):
    ctx = pl.BlockSpec((None, CTX_LEN, width), lambda b, s: (b, 0, 0))
    fwd = pl.BlockSpec((None, GROUP, width), lambda b, s: (b, jnp.maximum(s - 1, 0), 0))
    bwd = pl.BlockSpec((None, GROUP, width), lambda b, s: (b, ngroups - 1 - jnp.maximum(s - 1, 0), 0))
    return ctx, fwd, bwd


def _dn_scan(qkv_c, bg_c, qkv_l, bg_l):
    nb, seq = qkv_l.shape[0], qkv_l.shape[1]
    ng = seq // GROUP
    c768, f768, b768 = _scan_specs(768, ng)
    c128, f128, b128 = _scan_specs(LANES, ng)
    c256, f256, b256 = _scan_specs(256, ng)
    return pl.pallas_call(
        _dn_kernel,
        out_shape=(jax.ShapeDtypeStruct((nb, CTX_LEN, 256), F32), jax.ShapeDtypeStruct((nb, CTX_LEN, 256), F32),
                   jax.ShapeDtypeStruct((nb, seq, 256), F32), jax.ShapeDtypeStruct((nb, seq, 256), F32)),
        grid=(nb, ng + 1),
        in_specs=[c768, c128, f768, f128, b768, b128],
        out_specs=(c256, c256, f256, b256),
        scratch_shapes=[pltpu.VMEM((256, 256), F32), pltpu.VMEM((256, 256), F32)],
        compiler_params=_params(("parallel", "arbitrary")),
        name="dn_scan",
    )(qkv_c, bg_c, qkv_l, bg_l, qkv_l, bg_l)


def _gla_group(qk_f, v_f, m_f, qk_b, v_b, m_b, of_ref, ob_ref, sf_scr, sb_scr, w2, ba):
    t = qk_f.shape[0]
    n = t // CHUNK
    rows = _iota((CHUNK, 256), 0)
    lane_j = _iota((CHUNK, 256), 1) % CHUNK
    mask_k = _iota((256, 128), 0) // CHUNK == _iota((256, 128), 1) // GLA_DK
    mask_v = _iota((256, 256), 0) // CHUNK == _iota((256, 256), 1) // CHUNK
    mask_s = _iota((128, 256), 0) // GLA_DK == _iota((128, 256), 1) // CHUNK
    eye = (_iota((128, 128), 0) == _iota((128, 128), 1)).astype(F32)
    zero_b = jnp.zeros((), BF16)
    work = []
    for d, (qk, v, misc) in enumerate(((qk_f, v_f, m_f), (qk_b, v_b, m_b))):
        incl = (lane_j <= rows) if d == 0 else (lane_j >= rows)
        logit = _dot(misc[...], w2[:, d * 128:(d + 1) * 128], HI) + ba[:, d * 128:(d + 1) * 128]
        log_a = (jnp.minimum(logit, 0.0) - jnp.log1p(jnp.exp(-jnp.abs(logit)))) * (1.0 / GLA_TAU)
        rt = _iota((t, t), 0)
        ct = _iota((t, t), 1)
        cum = ((rt // CHUNK == ct // CHUNK) & ((ct <= rt) if d == 0 else (ct >= rt))).astype(F32)
        b_all = _mask_dot(cum, log_a)
        mid = CHUNK // 2 - 1 if d == 0 else CHUNK // 2
        last = CHUNK - 1 if d == 0 else 0
        for c in (range(n) if d == 0 else range(n - 1, -1, -1)):
            sl = slice(c * CHUNK, (c + 1) * CHUNK)
            q = qk[sl, 0:128].astype(F32) * (GLA_DK ** -0.5)
            k = qk[sl, 128:256].astype(F32)
            vb = v[sl, :]
            b = b_all[sl]
            b_mid = b[mid:mid + 1]
            b_last = b[last:last + 1]
            ke = jnp.where(mask_k, _tile4((k * jnp.exp(b_mid - b)).astype(BF16)), zero_b)
            a = jnp.where(incl, _dot_nt((q * jnp.exp(b - b_mid)).astype(BF16), ke), 0.0)
            o_intra = _dot(a.astype(BF16), jnp.where(mask_v, _tile4(vb), zero_b))
            ds = jnp.where(mask_s, _dot_tn((k * jnp.exp(b_last - b)).astype(BF16), vb), 0.0)
            dec_col = jnp.sum(eye * jnp.exp(b_last), axis=1, keepdims=True)
            work.append(dict(sl=sl, o_intra=o_intra, qb=(q * jnp.exp(b)).astype(BF16), ds=ds, dec_col=dec_col))
    state = [sf_scr[...], sb_scr[...]]
    out = (of_ref, ob_ref)
    for c in range(n):
        for d in range(2):
            w = work[d * n + c]
            out[d][w["sl"], :] = w["o_intra"] + _dot(w["qb"], state[d].astype(BF16))
            state[d] = w["dec_col"] * state[d] + w["ds"]
    sf_scr[...] = state[0]
    sb_scr[...] = state[1]


def _gla_kernel(cqk, cv, cm, lqkf, lvf, lmf, lqkb, lvb, lmb, w2_ref, ba_ref,
                ocf, ocb, olf, olb, sf_scr, sb_scr):
    step = pl.program_id(1)
    w2 = w2_ref[...]
    ba = ba_ref[...]

    @pl.when(step == 0)
    def _():
        sf_scr[...] = jnp.zeros_like(sf_scr)
        sb_scr[...] = jnp.zeros_like(sb_scr)
        _gla_group(cqk, cv, cm, cqk, cv, cm, ocf, ocb, sf_scr, sb_scr, w2, ba)

    @pl.when(step > 0)
    def _():
        _gla_group(lqkf, lvf, lmf, lqkb, lvb, lmb, olf, olb, sf_scr, sb_scr, w2, ba)


def _col_specs(width, col, ngroups):
    blk = col // width
    ctx = pl.BlockSpec((None, CTX_LEN, width), lambda b, s: (b, 0, blk))
    fwd = pl.BlockSpec((None, GROUP, width), lambda b, s: (b, jnp.maximum(s - 1, 0), blk))
    bwd = pl.BlockSpec((None, GROUP, width), lambda b, s: (b, ngroups - 1 - jnp.maximum(s - 1, 0), blk))
    return ctx, fwd, bwd


def _gla_scan(p_c, misc_c, p_l, misc_l, w2, ba):
    nb, seq = p_l.shape[0], p_l.shape[1]
    ng = seq // GROUP
    cqk, fqk, bqk = _col_specs(256, C_GQK, ng)
    cv, fv, bv = _col_specs(256, C_GV, ng)
    cm, fm, bm = _scan_specs(LANES, ng)
    c256, f256, b256 = _scan_specs(256, ng)
    const = lambda shape: pl.BlockSpec(shape, lambda b, s: (0, 0))
    return pl.pallas_call(
        _gla_kernel,
        out_shape=(jax.ShapeDtypeStruct((nb, CTX_LEN, 256), F32), jax.ShapeDtypeStruct((nb, CTX_LEN, 256), F32),
                   jax.ShapeDtypeStruct((nb, seq, 256), F32), jax.ShapeDtypeStruct((nb, seq, 256), F32)),
        grid=(nb, ng + 1),
        in_specs=[cqk, cv, cm, fqk, fv, fm, bqk, bv, bm, const((LANES, 256)), const((1, 256))],
        out_specs=(c256, c256, f256, b256),
        scratch_shapes=[pltpu.VMEM((128, 256), F32), pltpu.VMEM((128, 256), F32)],
        compiler_params=_params(("parallel", "arbitrary")),
        name="gla_scan",
    )(p_c, p_c, misc_c, p_l, p_l, misc_l, p_l, p_l, misc_l, w2, ba)


def _lane_group(width, group):
    return _iota((1, width), 1) // group


def _softmax_parts(parts):
    m = parts[0].max(axis=-1, keepdims=True)
    for s in parts[1:]:
        m = jnp.maximum(m, s.max(axis=-1, keepdims=True))
    ps = [jnp.exp(s - m) for s in parts]
    total = ps[0].sum(axis=-1, keepdims=True)
    for p in ps[1:]:
        total = total + p.sum(axis=-1, keepdims=True)
    return ps, 1.0 / total


def _na_kernel(q_ref, kl_ref, vl_ref, kc_ref, vc_ref, bias_ref, o_ref, *, grid_rows):
    blk = pl.program_id(1)
    key_row0 = jnp.clip(blk * NA_QROWS - NA_WIN_ROWS // 2, 0, grid_rows - NA_KROWS)
    start = pl.multiple_of(key_row0 * GRID_W, GRID_W)
    kw = kl_ref[pl.ds(start, NA_KROWS * GRID_W), :]
    vw = vl_ref[pl.ds(start, NA_KROWS * GRID_W), :]
    kc = kc_ref[...]
    vc = vc_ref[...]
    q = q_ref[...]
    head = _lane_group(256, NA_DH)
    acc = jnp.zeros(q.shape, F32)
    for h in range(N_HEADS):
        qh = jnp.where(head == h, q, jnp.zeros_like(q))
        (p_w, p_c), inv = _softmax_parts([_dot_nt(qh, kw) + bias_ref[h], _dot_nt(qh, kc)])
        o_h = (_dot(p_w.astype(BF16), vw) + _dot(p_c.astype(BF16), vc)) * inv
        acc = acc + jnp.where(head == h, o_h, 0.0)
    o_ref[...] = acc.astype(BF16)


def _na_attention(qkv_l, qkv_c, bias):
    nb, seq = qkv_l.shape[0], qkv_l.shape[1]
    tq = NA_QROWS * GRID_W
    nblk = seq // tq
    variant = lambda j: jnp.where(j == 0, 0, jnp.where(j == nblk - 1, 2, 1))
    kern = functools.partial(_na_kernel, grid_rows=seq // GRID_W)
    return pl.pallas_call(
        kern,
        out_shape=jax.ShapeDtypeStruct((nb, seq, 256), BF16),
        grid=(nb, nblk),
        in_specs=[pl.BlockSpec((None, tq, 256), lambda b, j: (b, j, 0)),
                  pl.BlockSpec((None, seq, 256), lambda b, j: (b, 0, 1)),
                  pl.BlockSpec((None, seq, 256), lambda b, j: (b, 0, 2)),
                  pl.BlockSpec((None, CTX_LEN, 256), lambda b, j: (b, 0, 1)),
                  pl.BlockSpec((None, CTX_LEN, 256), lambda b, j: (b, 0, 2)),
                  pl.BlockSpec((None, N_HEADS, tq, NA_KROWS * GRID_W), lambda b, j: (variant(j), 0, 0, 0))],
        out_specs=pl.BlockSpec((None, tq, 256), lambda b, j: (b, j, 0)),
        compiler_params=_params(("parallel", "arbitrary")),
        name="na_attention",
    )(qkv_l, qkv_l, qkv_l, qkv_c, qkv_c, bias)


def _na_ctx_kernel(q_ref, k_ref, v_ref, o_ref):
    q = q_ref[...]
    k = k_ref[...]
    v = v_ref[...]
    head = _lane_group(256, NA_DH)
    acc = jnp.zeros(q.shape, F32)
    for h in range(N_HEADS):
        qh = jnp.where(head == h, q, jnp.zeros_like(q))
        (p,), inv = _softmax_parts([_dot_nt(qh, k)])
        acc = acc + jnp.where(head == h, _dot(p.astype(BF16), v) * inv, 0.0)
    o_ref[...] = acc.astype(BF16)


def _na_ctx_attention(qkv_c):
    nb = qkv_c.shape[0]
    spec = lambda c: pl.BlockSpec((None, CTX_LEN, 256), lambda b: (b, 0, c))
    return pl.pallas_call(
        _na_ctx_kernel,
        out_shape=jax.ShapeDtypeStruct((nb, CTX_LEN, 256), BF16),
        grid=(nb,),
        in_specs=[spec(0), spec(1), spec(2)],
        out_specs=spec(0),
        compiler_params=_params(("parallel",)),
        name="na_ctx_attention",
    )(qkv_c, qkv_c, qkv_c)


def _na_bias_tables(rpb, grid_rows):
    a = np.arange(NA_QROWS)[:, None, None, None]
    qc = np.arange(GRID_W)[None, :, None, None]
    b = np.arange(NA_KROWS)[None, None, :, None]
    kc = np.arange(GRID_W)[None, None, None, :]
    shape = (NA_QROWS, GRID_W, NA_KROWS, GRID_W)
    half = NA_WIN_ROWS // 2
    n_r, n_c = 2 * NA_WIN_ROWS - 1, 2 * NA_WIN_COLS - 1
    cs = np.clip(qc - NA_WIN_COLS // 2, 0, GRID_W - NA_WIN_COLS)
    ci = np.clip(kc - qc + NA_WIN_COLS - 1, 0, n_c - 1)[0, :, 0, :]
    pick_c = (ci[..., None] == np.arange(n_c)).astype(np.float32)
    vis, pick_r = [], []
    for q_row0 in (0, half, grid_rows - NA_QROWS):
        k_row0 = int(np.clip(q_row0 - half, 0, grid_rows - NA_KROWS))
        qr = q_row0 + a
        kr = k_row0 + b
        rs = np.clip(qr - half, 0, grid_rows - NA_WIN_ROWS)
        vis.append(np.broadcast_to((kr >= rs) & (kr < rs + NA_WIN_ROWS) & (kc >= cs) & (kc < cs + NA_WIN_COLS), shape))
        ri = np.clip(kr - qr + NA_WIN_ROWS - 1, 0, n_r - 1)[:, 0, :, 0]
        pick_r.append((ri[..., None] == np.arange(n_r)).astype(np.float32))
    t = jnp.einsum('hrc,vabr,qkc->vhaqbk', rpb, jnp.asarray(np.stack(pick_r)), jnp.asarray(pick_c), precision=HI)
    t = jnp.where(jnp.asarray(np.stack(vis))[:, None], t, NEG)
    return t.reshape(3, N_HEADS, NA_QROWS * GRID_W, NA_KROWS * GRID_W)


DA_KEY_BLOCK = 256
DA_VROWS = 64 + 16


def _slab_reduce(x, op, slab=64):
    folded = op(x.reshape(x.shape[0] // slab, slab, x.shape[1]), axis=0)
    return op(folded, axis=0, keepdims=True)


def _da_kernel(*refs, lam_init, has_lat):
    if has_lat:
        q_ref, kl_ref, vtl_ref, kc_ref, vtc_ref, lam_ref, g_ref, o_ref, st_scr, p_scr = refs
    else:
        q_ref, kc_ref, vtc_ref, lam_ref, g_ref, o_ref = refs
    q = q_ref[...]
    tq = q.shape[0]
    lp = lam_ref[...]
    lam = (jnp.exp(jnp.sum(lp[0:1] * lp[1:2], axis=1, keepdims=True))
           - jnp.exp(jnp.sum(lp[2:3] * lp[3:4], axis=1, keepdims=True)) + lam_init)
    qmap = _lane_group(256, DIFF_DQK)
    zero = jnp.zeros_like(q)
    heads = range(N_HEADS)
    hs = [slice(h * 64, (h + 1) * 64) for h in heads]
    vrows = [slice(h * DA_VROWS, (h + 1) * DA_VROWS) for h in heads]
    qs = [jnp.concatenate([jnp.where(qmap == 2 * h, q, zero), jnp.where(qmap == 2 * h + 1, q, zero)], axis=0)
          for h in heads]
    kc = kc_ref[...]
    st = [_dot_nt(kc, qs[h]) for h in heads]
    m = [_slab_reduce(st[h], jnp.max) for h in heads]
    acc = [_dot(vtc_ref[vrows[h], :], jnp.exp2((st[h] - m[h]).astype(BF16))) for h in heads]
    if has_lat:
        n_blocks = kl_ref.shape[0] // DA_KEY_BLOCK

        def block(i):
            start = i * DA_KEY_BLOCK
            return pl.ds(start if isinstance(i, int) else pl.multiple_of(start, DA_KEY_BLOCK), DA_KEY_BLOCK)

        def scores(i, slot):
            kb = kl_ref[block(i), :]
            for h in heads:
                st_scr[slot, h] = _dot_nt(kb, qs[h])

        def softmax_stage(slot, m):
            m_new = [jnp.maximum(m[h], _slab_reduce(st_scr[slot, h], jnp.max)) for h in heads]
            alpha = [jnp.exp2(m[h] - m_new[h]) for h in heads]
            for h in heads:
                p_scr[slot, h] = jnp.exp2((st_scr[slot, h] - m_new[h]).astype(BF16))
            return m_new, alpha

        def value_stage(acc, alpha, slot, i):
            return [alpha[h] * acc[h] + _dot(vtl_ref[vrows[h], block(i)], p_scr[slot, h]) for h in heads]

        scores(0, 0)
        alpha = None
        for i in range(n_blocks):
            slot = i % 2
            if i + 1 < n_blocks:
                scores(i + 1, 1 - slot)
            m, alpha_i = softmax_stage(slot, m)
            if i > 0:
                acc = value_stage(acc, alpha, 1 - slot, i - 1)
            alpha = alpha_i
        acc = value_stage(acc, alpha, (n_blocks - 1) % 2, n_blocks - 1)
    outs = []
    for h in heads:
        den = acc[h][64:65, :]
        o_h = acc[h][0:64, 0:tq] * (1.0 / den[:, 0:tq]) - acc[h][0:64, tq:2 * tq] * (lam / den[:, tq:2 * tq])
        y_h = o_h * lax.rsqrt(jnp.mean(o_h * o_h, axis=0, keepdims=True) + EPS) * g_ref[hs[h], :]
        outs.append(y_h * (1.0 - lam_init))
    o_ref[...] = jnp.concatenate(outs, axis=0).T.astype(BF16)


def _da_attention(qkv_q, qkv_l, vt_l, qkv_c, vt_c, lam_params, gain_col, lam_init, tq):
    nb, nq = qkv_q.shape[0], qkv_q.shape[1]
    has_lat = qkv_l is not None
    full = lambda arr, c: pl.BlockSpec((None, arr.shape[1], 256), lambda b, j: (b, 0, c))
    full_t = lambda arr: pl.BlockSpec((None, arr.shape[1], arr.shape[2]), lambda b, j: (b, 0, 0))
    in_specs = [pl.BlockSpec((None, tq, 256), lambda b, j: (b, j, 0))]
    args = [qkv_q]
    if has_lat:
        in_specs += [full(qkv_l, 1), full_t(vt_l)]
        args += [qkv_l, vt_l]
    in_specs += [full(qkv_c, 1), full_t(vt_c),
                 pl.BlockSpec((4, DIFF_DQK), lambda b, j: (0, 0)), pl.BlockSpec((256, 1), lambda b, j: (0, 0))]
    args += [qkv_c, vt_c, lam_params, gain_col]
    scratch = []
    if has_lat:
        assert qkv_l.shape[1] % DA_KEY_BLOCK == 0
        scratch = [pltpu.VMEM((2, N_HEADS, DA_KEY_BLOCK, 2 * tq), F32),
                   pltpu.VMEM((2, N_HEADS, DA_KEY_BLOCK, 2 * tq), BF16)]
    return pl.pallas_call(
        functools.partial(_da_kernel, lam_init=lam_init, has_lat=has_lat),
        out_shape=jax.ShapeDtypeStruct((nb, nq, 256), BF16),
        grid=(nb, nq // tq),
        in_specs=in_specs,
        out_specs=pl.BlockSpec((None, tq, 256), lambda b, j: (b, j, 0)),
        scratch_shapes=scratch,
        compiler_params=_params(("parallel", "arbitrary")),
        name="diff_attention",
    )(*args)


def _merge_kernel(x_ref, g1_ref, dnf_ref, dnb_ref, z_ref, glf_ref, glb_ref, r_ref, na_ref, df_ref, gate_ref,
                  bgate_ref, dng_ref, glg_ref, wb_ref, wo_ref, o_ref):
    def head_norm(o, gate, g):
        return ((o * lax.rsqrt(_group_mean(o * o, 64) + EPS) * g) * _silu(gate.astype(F32))).astype(BF16)

    ys = (head_norm(dnf_ref[...] + dnb_ref[...], z_ref[...], dng_ref[...]), na_ref[...],
          head_norm(glf_ref[...] + glb_ref[...], r_ref[...], glg_ref[...]), df_ref[...])
    acc = jnp.zeros(x_ref.shape, F32)
    for g in range(N_BRANCH):
        gate = _sigmoid(gate_ref[:, g * D_MODEL:(g + 1) * D_MODEL].astype(F32) + bgate_ref[g:g + 1, :])
        acc = acc + gate * _dot(ys[g], wb_ref[g])
    o_ref[...] = x_ref[...] + g1_ref[...] * _dot(acc.astype(BF16), wo_ref[...])


def _merge(x2, g1, dn_f, dn_b, gl_f, gl_b, y_na, y_df, p, b_gate, dn_g, gl_g, w_branch, w_out, rows_per_mod, tm):
    rows = x2.shape[0]
    row = lambda w, c=0: pl.BlockSpec((tm, w), lambda i: (i, c))
    const = lambda shape: pl.BlockSpec(shape, lambda i: (0,) * len(shape))
    return pl.pallas_call(
        _merge_kernel,
        out_shape=jax.ShapeDtypeStruct((rows, D_MODEL), F32),
        grid=(rows // tm,),
        in_specs=[row(D_MODEL),
                  pl.BlockSpec((None, 1, D_MODEL), lambda i: ((i * tm) // rows_per_mod, 0, 0)),
                  row(256), row(256), row(256, C_DNZ // 256),
                  row(256), row(256), row(256, C_GR // 256),
                  row(256), row(256), row(N_BRANCH * D_MODEL, C_GATE // (N_BRANCH * D_MODEL)),
                  const((N_BRANCH, D_MODEL)), const((1, 256)), const((1, 256)),
                  const((N_BRANCH, BRANCH_W, D_MODEL)), const((D_MODEL, D_MODEL))],
        out_specs=row(D_MODEL),
        compiler_params=_params(("parallel",)),
        name="merge",
    )(x2, g1, dn_f, dn_b, p, gl_f, gl_b, p, y_na, y_df, p, b_gate, dn_g, gl_g, w_branch, w_out)


FFN_COLS = 256


def _ffn_kernel(x_ref, xp_ref, xn_ref, gain_ref, sc_ref, sh_ref, g2_ref, wu_ref, wv_ref, cw_ref, cb_ref, wo_ref,
                o_ref, u_scr, act_scr, *, tm, seq_len):
    i = pl.program_id(0)
    first = (i * tm) % seq_len == 0
    last = ((i + 1) * tm) % seq_len == 0

    def norm(x):
        y = x * lax.rsqrt(jnp.mean(x * x, axis=-1, keepdims=True) + EPS)
        return (y * gain_ref[...]) * sc_ref[...] + sh_ref[...]

    x = x_ref[...]
    h = norm(x)
    he = jnp.concatenate([norm(xp_ref[...]), h, norm(xn_ref[...])], axis=0).astype(BF16)
    h = h.astype(BF16)
    rows = _iota((tm + 2 * SUBLANES, 1), 0)
    keep = jnp.logical_not((first & (rows < SUBLANES)) | (last & (rows >= tm + SUBLANES)))
    u_scr[...] = jnp.where(keep, _dot(he, wu_ref[...]), 0.0)
    for c in range(D_FF // FFN_COLS):
        cols = slice(c * FFN_COLS, (c + 1) * FFN_COLS)
        conv =(u_scr[SUBLANES - 1:SUBLANES - 1 + tm, cols] * cw_ref[0:1, cols]
                + u_scr[SUBLANES:SUBLANES + tm, cols] * cw_ref[1:2, cols]
                + u_scr[SUBLANES + 1:SUBLANES + 1 + tm, cols] * cw_ref[2:3, cols]) + cb_ref[:, cols]
        act_scr[:, cols] = (_silu(conv) * _dot(h, wv_ref[:, cols])).astype(BF16)
    o_ref[...] = x + g2_ref[...] * _dot(act_scr[...], wo_ref[...])


def _ffn(x2, gain, scale1p, shift, g2, w_u, w_v, conv_w, conv_b, w_o, rows_per_mod, seq_len, tm):
    rows = x2.shape[0]
    nb8 = rows // SUBLANES
    t8 = tm // SUBLANES
    const = lambda shape: pl.BlockSpec(shape, lambda i: (0,) * len(shape))
    weight = lambda shape: pl.BlockSpec(shape, lambda i: (0,) * len(shape), pipeline_mode=pl.Buffered(1))
    mod = pl.BlockSpec((None, 1, D_MODEL), lambda i: ((i * tm) // rows_per_mod, 0, 0))
    return pl.pallas_call(
        functools.partial(_ffn_kernel, tm=tm, seq_len=seq_len),
        out_shape=jax.ShapeDtypeStruct((rows, D_MODEL), F32),
        grid=(rows // tm,),
        in_specs=[pl.BlockSpec((tm, D_MODEL), lambda i: (i, 0)),
                  pl.BlockSpec((SUBLANES, D_MODEL), lambda i: (jnp.maximum(i * t8 - 1, 0), 0)),
                  pl.BlockSpec((SUBLANES, D_MODEL), lambda i: (jnp.minimum((i + 1) * t8, nb8 - 1), 0)),
                  const((1, D_MODEL)), mod, mod, mod,
                  weight((D_MODEL, D_FF)), weight((D_MODEL, D_FF)), const((3, D_FF)), const((1, D_FF)),
                  weight((D_FF, D_MODEL))],
        out_specs=pl.BlockSpec((tm, D_MODEL), lambda i: (i, 0)),
        scratch_shapes=[pltpu.VMEM((tm + 2 * SUBLANES, D_FF), F32), pltpu.VMEM((tm, D_FF), BF16)],
        compiler_params=_params(("parallel",)),
        name="conv_ffn",
    )(x2, x2, x2, gain, scale1p, shift, g2, w_u, w_v, conv_w, conv_b, w_o)


def _inproj_columns():
    widths = (768, 256, 8, 8, 768, 128, 128, 256, 256, 32, 768, 4096)
    o = np.concatenate([[0], np.cumsum(widths)])
    src = np.zeros(P_WIDTH, np.int64)
    sign = np.zeros(P_WIDTH, np.float32)

    def put(dst, cols, sgn=None):
        src[dst:dst + len(cols)] = cols
        sign[dst:dst + len(cols)] = 1.0 if sgn is None else sgn

    put(C_DN, np.arange(o[0], o[1]))
    put(C_NA, np.arange(o[4], o[5]))
    put(C_DF, np.arange(o[10], o[11]))
    put(C_DNZ, np.arange(o[1], o[2]))
    quarter = DIFF_DQK // 4
    perm = np.concatenate([np.arange(quarter, 2 * quarter), np.arange(0, quarter),
                           np.arange(3 * quarter, 4 * quarter), np.arange(2 * quarter, 3 * quarter)])
    sgn = np.concatenate([-np.ones(quarter), np.ones(quarter), -np.ones(quarter), np.ones(quarter)])
    j = np.arange(512)
    put(C_DFROT, o[10] + (j // DIFF_DQK) * DIFF_DQK + perm[j % DIFF_DQK], sgn[j % DIFF_DQK])
    put(C_GQK, np.arange(o[5], o[7]))
    put(C_GV, np.arange(o[7], o[8]))
    put(C_GR, np.arange(o[8], o[9]))
    put(C_MISC, np.concatenate([np.arange(o[2], o[4]), np.arange(o[9], o[10])]))
    put(C_GATE, np.arange(o[11], o[12]))
    return src, sign, perm


def _inproj_weights(w_in):
    w = w_in.astype(BF16)
    widths = (768, 256, 8, 8, 768, 128, 128, 256, 256, 32, 768, 4096)
    o = np.concatenate([[0], np.cumsum(widths)])
    seg = lambda a, b: w[:, :, a:b]
    zeros = lambda n: jnp.zeros(w.shape[:2] + (n,), BF16)

    def rot(t):
        t6 = t.reshape(t.shape[:2] + (256 // DIFF_DQK, 2, 2, DIFF_DQK // 4))
        sign = jnp.asarray([-1.0, 1.0], BF16).reshape(1, 1, 1, 1, 2, 1)
        return (jnp.flip(t6, axis=4) * sign).reshape(t.shape)

    pieces = [seg(o[0], o[1]), seg(o[4], o[5]), seg(o[10], o[11]), seg(o[1], o[2]),
              rot(seg(o[10], o[10] + 256)), rot(seg(o[10] + 256, o[10] + 512)),
              seg(o[5], o[7]), seg(o[7], o[8]), seg(o[8], o[9]),
              seg(o[2], o[4]), seg(o[9], o[10]), zeros(C_GATE - C_MISC - 48), seg(o[11], o[12])]
    return jnp.concatenate(pieces, axis=-1)


def _rope_tables(length):
    t = jnp.arange(length)
    row = (t // GRID_W).astype(F32)
    col = (t % GRID_W).astype(F32)
    n_freq = DIFF_DQK // 4
    inv_freq = jnp.power(jnp.float32(ROPE_THETA), -jnp.arange(n_freq, dtype=F32) / n_freq)
    ang_r = row[:, None] * inv_freq
    ang_c = col[:, None] * inv_freq
    ang = jnp.concatenate([ang_r, ang_r, ang_c, ang_c], axis=-1)
    reps = 256 // DIFF_DQK
    return jnp.tile(jnp.cos(ang), (1, reps)), jnp.tile(jnp.sin(ang), (1, reps))


def _values_transposed(v):
    nb, t, _ = v.shape
    v4 = v.reshape(nb, t, N_HEADS, 64)
    ones = jnp.ones((nb, t, N_HEADS, DA_VROWS - 64), v.dtype)
    return jnp.swapaxes(jnp.concatenate([v4, ones], axis=-1).reshape(nb, t, N_HEADS * DA_VROWS), 1, 2)


def _pad_lanes(v, offset, width):
    return jnp.zeros((1, width), F32).at[0, offset:offset + v.shape[0]].set(v)


def kernel(x, c, ctx, c_ctx, w_mod, b_mod, norm1_g, norm2_g, w_in, b_gate, dn_conv, dn_a_log, dn_dt_bias, dn_norm_g,
           na_q_norm, na_k_norm, na_rpb, gla_w_a2, gla_b_a, gla_norm_g, df_q_norm, df_k_norm, df_lambda, df_norm_g,
           w_branch, w_out, ffn_w_in, ffn_conv_w, ffn_conv_b, ffn_w_out):
    nb, seq, _ = x.shape
    n_lat, n_ctx = nb * seq, nb * CTX_LEN
    tm_lat, tm_ctx = 512, 256

    cc = jnp.zeros((16, D_MODEL), F32).at[0:nb].set(c).at[nb].set(c_ctx)
    mod = _modulation(cc, w_mod, b_mod)

    src, sign, perm = _inproj_columns()
    w_in_p = _inproj_weights(w_in)
    w_branch_b = w_branch.astype(BF16)
    w_out_b = w_out.astype(BF16)
    w_u = ffn_w_in[:, :, 0:D_FF].astype(BF16)
    w_v = ffn_w_in[:, :, D_FF:2 * D_FF].astype(BF16)
    w_o = ffn_w_out.astype(BF16)
    cos, sin = _rope_tables(seq)

    x2 = x.reshape(n_lat, D_MODEL)
    xc2 = ctx.reshape(n_ctx, D_MODEL)
    for li in range(DEPTH):
        with_ctx = li < DEPTH - 1
        lam_init = 0.8 - 0.6 * math.exp(-0.3 * li)
        m = mod[li]
        part = lambda k: m[:, k * D_MODEL:(k + 1) * D_MODEL]
        lat = lambda v: v[0:nb].reshape(nb, 1, D_MODEL)
        cx = lambda v: v[nb:nb + 1].reshape(1, 1, D_MODEL)
        sh1, sc1, g1, sh2, sc2, g2 = (part(k) for k in range(6))
        n1 = norm1_g[li].reshape(1, D_MODEL)
        n2 = norm2_g[li].reshape(1, D_MODEL)

        p_l, misc_l = _inproj(x2, n1, lat(1.0 + sc1), lat(sh1), w_in_p[li], seq, 2048)
        p_c, misc_c = _inproj(xc2, n1, cx(1.0 + sc1), cx(sh1), w_in_p[li], n_ctx, 2048)

        tile4 = lambda v: jnp.tile(v, 256 // v.shape[0]).reshape(1, 256)
        vecs = (dn_conv[li], _pad_lanes(dn_a_log[li].reshape(-1), 2 * N_HEADS, LANES),
                _pad_lanes(dn_dt_bias[li].reshape(-1), 2 * N_HEADS, LANES),
                tile4(na_q_norm[li]), tile4(na_k_norm[li]), tile4(df_q_norm[li]), tile4(df_k_norm[li]),
                tile4(df_q_norm[li][perm]), tile4(df_k_norm[li][perm]))
        dn_l, bg_l, na_l, df_l, vt_l = _prep(p_l, misc_l, cos, sin, vecs, tp=tm_lat, seq_len=seq, rope=True)
        dn_c, bg_c, na_c, df_c, vt_c = _prep(p_c, misc_c, cos, sin, vecs, tp=tm_ctx, seq_len=CTX_LEN, rope=False)

        b3 = lambda a, t: a.reshape(nb, t, a.shape[-1])
        dn_cf, dn_cb, dn_lf, dn_lb = _dn_scan(b3(dn_c, CTX_LEN), b3(bg_c, CTX_LEN), b3(dn_l, seq), b3(bg_l, seq))

        w2 = jnp.zeros((LANES, 256), F32)
        a1_off = 4 * N_HEADS
        w2 = w2.at[a1_off:a1_off + GLA_RANK, 0:128].set(gla_w_a2[li, 0])
        w2 = w2.at[a1_off + GLA_RANK:a1_off + 2 * GLA_RANK, 128:256].set(gla_w_a2[li, 1])
        gl_cf, gl_cb, gl_lf, gl_lb = _gla_scan(b3(p_c, CTX_LEN), b3(misc_c, CTX_LEN), b3(p_l, seq), b3(misc_l, seq),
                                               w2, gla_b_a[li].reshape(1, 256))

        y_na = _na_attention(b3(na_l, seq), b3(na_c, CTX_LEN), _na_bias_tables(na_rpb[li], seq // GRID_W))
        df_gain = tile4(df_norm_g[li]).reshape(256, 1)
        y_df = _da_attention(b3(df_l, seq), b3(df_l, seq), vt_l, b3(df_c, CTX_LEN), vt_c, df_lambda[li],
                             df_gain, lam_init, 512)

        flat = lambda a: a.reshape(-1, a.shape[-1])
        gains = (b_gate[li], tile4(dn_norm_g[li]), tile4(gla_norm_g[li]), w_branch_b[li], w_out_b[li])
        ffn_w = (w_u[li], w_v[li], ffn_conv_w[li], ffn_conv_b[li].reshape(1, D_FF), w_o[li])
        x2 = _merge(x2, lat(g1), flat(dn_lf), flat(dn_lb), flat(gl_lf), flat(gl_lb), flat(y_na), flat(y_df), p_l,
                    *gains, seq, tm_lat)
        x2 = _ffn(x2, n2, lat(1.0 + sc2), lat(sh2), lat(g2), *ffn_w, seq, seq, tm_lat)
        if with_ctx:
            yc_na = _na_ctx_attention(b3(na_c, CTX_LEN))
            yc_df = _da_attention(b3(df_c, CTX_LEN), None, None, b3(df_c, CTX_LEN), vt_c, df_lambda[li],
                                  df_gain, lam_init, 128)
            xc2 = _merge(xc2, cx(g1), flat(dn_cf), flat(dn_cb), flat(gl_cf), flat(gl_cb), flat(yc_na), flat(yc_df),
                         p_c, *gains, n_ctx, tm_ctx)
            xc2 = _ffn(xc2, n2, cx(1.0 + sc2), cx(sh2), cx(g2), *ffn_w, n_ctx, CTX_LEN, tm_ctx)
    return x2.reshape(nb, seq, D_MODEL)
```

```python
import functools
import math

import numpy as np
import jax
import jax.numpy as jnp
from jax import lax
from jax.experimental import pallas as pl
from jax.experimental.pallas import tpu as pltpu

F32 = jnp.float32
BF16 = jnp.bfloat16
HI = lax.Precision.HIGHEST

D_MODEL = 1024
DEPTH = 4
GRID_W = 64
CTX_LEN = 256
N_BRANCH = 4
BRANCH_W = 256
CHUNK = 64
N_HEADS = 4
DN_DK = 64
DN_CONV = 3
NA_DH = 64
NA_WIN_ROWS = 8
NA_WIN_COLS = 16
GLA_DK = 32
GLA_RANK = 16
GLA_TAU = 16.0
DIFF_DQK = 32
ROPE_THETA = 10000.0
D_FF = 2816
EPS = 1e-6
NEG = -1e30
LOG2E = 1.4426950408889634

LANES = 128
SUBLANES = 8
HALO = 16
VMEM_LIMIT = 56 * 1024 * 1024

C_DN = 0
C_NA = 768
C_DF = 1536
C_DNZ = 2304
C_DFROT = 2560
C_GQK = 3072
C_GV = 3328
C_GR = 3584
C_MISC = 3840
C_GATE = 4096
P_WIDTH = 8192

GROUP = 4 * CHUNK
NA_QROWS = 4
NA_KROWS = 12

ROWS_LAT = 512
ROWS_CTX = CTX_LEN
INPROJ_ROWS = 2048
INPROJ_COLS = 2048
DA_QUERIES_LAT = 512
DA_QUERIES_CTX = 128


def _dot(a, b, prec=None):
    return jnp.dot(a, b, preferred_element_type=F32, precision=prec)


def _dot_nt(a, b, prec=None):
    return lax.dot_general(a, b, (((1,), (1,)), ((), ())), preferred_element_type=F32, precision=prec)


def _dot_tn(a, b, prec=None):
    return lax.dot_general(a, b, (((0,), (0,)), ((), ())), preferred_element_type=F32, precision=prec)


def _iota(shape, dim):
    return lax.broadcasted_iota(jnp.int32, shape, dim)


def _group_mask(rows, cols, rgroup, cgroup):
    return (_iota((rows, cols), 0) // rgroup == _iota((rows, cols), 1) // cgroup).astype(F32)


def _group_mean(xsq, width):
    n = xsq.shape[-1]
    ones = _group_mask(n, n, width, width).astype(BF16)
    return _dot_mask(xsq, ones) * (1.0 / width)


def _split(x, terms):
    parts = []
    for _ in range(terms):
        piece = x.astype(BF16)
        parts.append(piece)
        x = x - piece.astype(F32)
    return parts


def _dot_mask(x, mask01, terms=2):
    m = mask01.astype(BF16)
    return sum(_dot(piece, m) for piece in _split(x, terms))


def _mask_dot(mask01, x, terms=3):
    m = mask01.astype(BF16)
    return sum(_dot(m, piece) for piece in _split(x, terms))


def _sigmoid(x):
    return 0.5 * jnp.tanh(0.5 * x) + 0.5


def _silu(x):
    return x * _sigmoid(x)


def _softplus(x):
    return jnp.maximum(x, 0.0) + jnp.log1p(jnp.exp(-jnp.abs(x)))


def _params(sem):
    return pltpu.CompilerParams(dimension_semantics=sem, vmem_limit_bytes=VMEM_LIMIT)


def _mod_kernel(c_ref, w_ref, b_ref, o_ref):
    o_ref[...] = _dot(_silu(c_ref[...]), w_ref[...], HI) + b_ref[...]


def _modulation(cc, w_mod, b_mod):
    tn = 1024
    return pl.pallas_call(
        _mod_kernel,
        out_shape=jax.ShapeDtypeStruct((DEPTH, 16, 6 * D_MODEL), F32),
        grid=(DEPTH, 6 * D_MODEL // tn),
        in_specs=[pl.BlockSpec((16, D_MODEL), lambda l, j: (0, 0)),
                  pl.BlockSpec((None, D_MODEL, tn), lambda l, j: (l, 0, j)),
                  pl.BlockSpec((None, 1, tn), lambda l, j: (l, 0, j))],
        out_specs=pl.BlockSpec((None, 16, tn), lambda l, j: (l, 0, j)),
        compiler_params=_params(("arbitrary", "arbitrary")),
        name="modulation",
    )(cc, w_mod, b_mod.reshape(DEPTH, 1, 6 * D_MODEL))


def _inproj_kernel(x_ref, g_ref, sc_ref, sh_ref, w_ref, o_ref, misc_ref, h_scr, *, tn):
    j = pl.program_id(1)

    @pl.when(j == 0)
    def _():
        x = x_ref[...]
        y = x * lax.rsqrt(jnp.mean(x * x, axis=-1, keepdims=True) + EPS)
        h_scr[...] = ((y * g_ref[...]) * sc_ref[...] + sh_ref[...]).astype(BF16)

    acc = _dot(h_scr[...], w_ref[...])
    o_ref[...] = acc.astype(BF16)

    @pl.when(j == C_MISC // tn)
    def _():
        misc_ref[...] = acc[:, C_MISC % tn:C_MISC % tn + LANES]


def _inproj(x2, gain, scale1p, shift, w, li, rows_per_mod, tm):
    rows = x2.shape[0]
    tn = INPROJ_COLS
    mod_spec = pl.BlockSpec((None, 1, D_MODEL), lambda i, j: ((i * tm) // rows_per_mod, 0, 0))
    return pl.pallas_call(
        functools.partial(_inproj_kernel, tn=tn),
        out_shape=(jax.ShapeDtypeStruct((rows, P_WIDTH), BF16), jax.ShapeDtypeStruct((rows, LANES), F32)),
        grid=(rows // tm, P_WIDTH // tn),
        in_specs=[pl.BlockSpec((tm, D_MODEL), lambda i, j: (i, 0)),
                  pl.BlockSpec((1, D_MODEL), lambda i, j: (0, 0)),
                  mod_spec, mod_spec,
                  pl.BlockSpec((None, D_MODEL, tn), lambda i, j: (li, 0, j))],
        out_specs=(pl.BlockSpec((tm, tn), lambda i, j: (i, j)), pl.BlockSpec((tm, LANES), lambda i, j: (i, 0))),
        scratch_shapes=[pltpu.VMEM((tm, D_MODEL), BF16)],
        compiler_params=_params(("parallel", "arbitrary")),
        name="inproj",
    )(x2, gain, scale1p, shift, w)


def _prep_kernel(dn_ref, dnp_ref, dnn_ref, misc_ref, na_ref, df_ref, dfrot_ref, cos_ref, sin_ref,
                 convw_ref, alog_ref, dt_ref, naq_ref, nak_ref, dfq_ref, dfk_ref, dfqr_ref, dfkr_ref,
                 dn_o, bg_o, na_o, df_o, vt_o, xe_scr, *, tp, seq_len, rope):
    i = pl.program_id(0)
    first = (i * tp) % seq_len == 0
    last = ((i + 1) * tp) % seq_len == 0
    halo_prev = dnp_ref[...].astype(F32)[HALO - SUBLANES:HALO]
    halo_next = dnn_ref[...].astype(F32)[0:SUBLANES]
    xe_scr[0:SUBLANES, :] = jnp.where(first, 0.0, halo_prev)
    xe_scr[SUBLANES:SUBLANES + tp, :] = dn_ref[...].astype(F32)
    xe_scr[SUBLANES + tp:2 * SUBLANES + tp, :] = jnp.where(last, 0.0, halo_next)
    cw = convw_ref[...]
    conv = (xe_scr[SUBLANES - 1:SUBLANES - 1 + tp, :] * cw[0:1]
            + xe_scr[SUBLANES:SUBLANES + tp, :] * cw[1:2]
            + xe_scr[SUBLANES + 1:SUBLANES + 1 + tp, :] * cw[2:3])
    s = _silu(conv)
    q, k, v = s[:, 0:256], s[:, 256:512], s[:, 512:768]
    qn = q * lax.rsqrt(_group_mean(q * q, DN_DK) * DN_DK + EPS) * (DN_DK ** -0.5)
    kn = k * lax.rsqrt(_group_mean(k * k, DN_DK) * DN_DK + EPS)
    dn_o[:, 0:256] = qn
    dn_o[:, 256:512] = kn
    dn_o[:, 512:768] = v
    m = misc_ref[...]
    beta = _sigmoid(m)
    g = -jnp.exp(alog_ref[...]) * _softplus(m + dt_ref[...])
    bg_o[...] = jnp.where(_iota(m.shape, 1) < 2 * N_HEADS, beta, g)
    q, k = na_ref[:, 0:256].astype(F32), na_ref[:, 256:512].astype(F32)
    qn = q * lax.rsqrt(_group_mean(q * q, NA_DH) + EPS) * naq_ref[...] * (NA_DH ** -0.5 * LOG2E)
    kn = k * lax.rsqrt(_group_mean(k * k, NA_DH) + EPS) * nak_ref[...]
    na_o[:, 0:256] = qn.astype(BF16)
    na_o[:, 256:512] = kn.astype(BF16)
    na_o[:, 512:768] = na_ref[:, 512:768]
    q, k = df_ref[:, 0:256].astype(F32), df_ref[:, 256:512].astype(F32)
    rq = lax.rsqrt(_group_mean(q * q, DIFF_DQK) + EPS)
    rk = lax.rsqrt(_group_mean(k * k, DIFF_DQK) + EPS)
    qn = q * rq * dfq_ref[...]
    kn = k * rk * dfk_ref[...]
    if rope:
        xr = dfrot_ref[...].astype(F32)
        cos, sin = cos_ref[...], sin_ref[...]
        qn = qn * cos + (xr[:, 0:256] * rq * dfqr_ref[...]) * sin
        kn = kn * cos + (xr[:, 256:512] * rk * dfkr_ref[...]) * sin
    df_o[:, 0:256] = (qn * (DIFF_DQK ** -0.5 * LOG2E)).astype(BF16)
    df_o[:, 256:512] = kn.astype(BF16)
    df_o[:, 512:768] = df_ref[:, 512:768]
    vt = df_ref[:, 512:768].astype(F32).T
    ones = jnp.ones((DA_VROWS - 64, tp), F32)
    vt_o[...] = jnp.concatenate([piece for h in range(N_HEADS) for piece in (vt[h * 64:(h + 1) * 64], ones)],
                                axis=0).astype(BF16)


def _prep(p, misc, cos, sin, vecs, *, tp, seq_len, rope):
    rows = p.shape[0]
    nb8 = rows // HALO
    t8 = tp // HALO
    nseq_tiles = max(seq_len // tp, 1)
    row_vec = lambda w: pl.BlockSpec((1, w), lambda i: (0, 0))
    kern = functools.partial(_prep_kernel, tp=tp, seq_len=seq_len, rope=rope)
    return pl.pallas_call(
        kern,
        out_shape=(jax.ShapeDtypeStruct((rows, 768), F32), jax.ShapeDtypeStruct((rows, LANES), F32),
                   jax.ShapeDtypeStruct((rows, 768), BF16), jax.ShapeDtypeStruct((rows, 768), BF16),
                   jax.ShapeDtypeStruct((rows // seq_len, N_HEADS * DA_VROWS, seq_len), BF16)),
        grid=(rows // tp,),
        in_specs=[pl.BlockSpec((tp, 768), lambda i: (i, C_DN // 768)),
                  pl.BlockSpec((HALO, 768), lambda i: (jnp.maximum(i * t8 - 1, 0), C_DN // 768)),
                  pl.BlockSpec((HALO, 768), lambda i: (jnp.minimum((i + 1) * t8, nb8 - 1), C_DN // 768)),
                  pl.BlockSpec((tp, LANES), lambda i: (i, 0)),
                  pl.BlockSpec((tp, 768), lambda i: (i, C_NA // 768)),
                  pl.BlockSpec((tp, 768), lambda i: (i, C_DF // 768)),
                  pl.BlockSpec((tp, 512), lambda i: (i, C_DFROT // 512)),
                  pl.BlockSpec((tp, 256), lambda i: (i % nseq_tiles, 0)),
                  pl.BlockSpec((tp, 256), lambda i: (i % nseq_tiles, 0)),
                  pl.BlockSpec((DN_CONV, 768), lambda i: (0, 0)),
                  row_vec(LANES), row_vec(LANES),
                  row_vec(256), row_vec(256), row_vec(256), row_vec(256), row_vec(256), row_vec(256)],
        out_specs=(pl.BlockSpec((tp, 768), lambda i: (i, 0)), pl.BlockSpec((tp, LANES), lambda i: (i, 0)),
                   pl.BlockSpec((tp, 768), lambda i: (i, 0)), pl.BlockSpec((tp, 768), lambda i: (i, 0)),
                   pl.BlockSpec((None, N_HEADS * DA_VROWS, tp), lambda i: (i // nseq_tiles, 0, i % nseq_tiles))),
        scratch_shapes=[pltpu.VMEM((tp + 2 * SUBLANES, 768), F32)],
        compiler_params=_params(("parallel",)),
        name="prep",
    )(p, p, p, misc, p, p, p, cos, sin, *vecs)


def _tile4(x):
    return jnp.concatenate([x, x, x, x], axis=0)


def _dn_group(qkv_f, bg_f, qkv_b, bg_b, of_ref, ob_ref, sf_scr, sb_scr):
    t = qkv_f.shape[0]
    n = t // CHUNK
    rows = _iota((CHUNK, 256), 0)
    lane_j = _iota((CHUNK, 256), 1) % CHUNK
    eye_cat = (lane_j == rows).astype(F32)
    mask_bd = _iota((256, 256), 0) // CHUNK == _iota((256, 256), 1) // CHUNK
    zero_b = jnp.zeros((), BF16)

    def bd(x):
        return jnp.where(mask_bd, _tile4(x.astype(BF16)), zero_b)

    work = []
    for d, (qkv, bg) in enumerate(((qkv_f, bg_f), (qkv_b, bg_b))):
        incl = (lane_j <= rows) if d == 0 else (lane_j >= rows)
        strict = (lane_j < rows) if d == 0 else (lane_j > rows)
        srow = _iota((LANES, 512), 0)
        scol = _iota((LANES, 512), 1)
        sel = (srow == (scol // 256) * 2 * N_HEADS + d * N_HEADS + (scol % 256) // CHUNK).astype(F32)
        e = _dot_mask(bg[...], sel, terms=3)
        rt = _iota((t, t), 0)
        ct = _iota((t, t), 1)
        cum = ((rt // CHUNK == ct // CHUNK) & ((ct <= rt) if d == 0 else (ct >= rt))).astype(F32)
        gc_all = _mask_dot(cum, e[:, 256:512])
        for c in (range(n) if d == 0 else range(n - 1, -1, -1)):
            sl = slice(c * CHUNK, (c + 1) * CHUNK)
            q, k, v = qkv[sl, 0:256], qkv[sl, 256:512], qkv[sl, 512:768]
            beta, gc = e[sl, 0:256], gc_all[sl]
            gc_row = jnp.sum(eye_cat * gc, axis=0, keepdims=True)
            decay = jnp.exp(jnp.where(incl, gc - gc_row, -jnp.inf))
            gc_last = gc[CHUNK - 1:CHUNK] if d == 0 else gc[0:1]
            kb = k * beta
            gram = _dot_nt(jnp.concatenate([kb, q], axis=0).astype(BF16), bd(k))
            n_cat = jnp.where(strict, gram[0:CHUNK] * decay, 0.0)
            egc = jnp.exp(gc)
            work.append(dict(
                d=d, sl=sl, n_cat=n_cat, a_cat=jnp.where(incl, gram[CHUNK:2 * CHUNK] * decay, 0.0).astype(BF16),
                rhs_u=bd(v * beta), rhs_w=bd(kb * egc), qd=(q * egc).astype(BF16),
                kd=(k * jnp.exp(gc_last - gc)).astype(BF16), s_decay=jnp.exp(gc_last),
                p=eye_cat - jnp.where(rows // 2 == lane_j // 2, n_cat, 0.0)))
    size = 2
    while size < CHUNK:
        off = (rows // (2 * size) == lane_j // (2 * size)) & (rows // size != lane_j // size)
        half = [_dot(w["p"].astype(BF16), bd(jnp.where(off, w["n_cat"], 0.0))) for w in work]
        for w, hx in zip(work, half):
            w["p"] = w["p"] - _dot(hx.astype(BF16), bd(w["p"]))
        size *= 2
    for w in work:
        pb = w["p"].astype(BF16)
        w["u"] = _dot(pb, w["rhs_u"])
        w["wq"] = jnp.concatenate([_dot(pb, w["rhs_w"]).astype(BF16), w["qd"]], axis=0)
    state = [sf_scr[...], sb_scr[...]]
    out = (of_ref, ob_ref)
    for c in range(n):
        for d in range(2):
            w = work[d * n + c]
            ws = _dot(w["wq"], state[d].astype(BF16))
            v_new = w["u"] - ws[0:CHUNK]
            out[d][w["sl"], :] = ws[CHUNK:2 * CHUNK] + _dot(w["a_cat"], bd(v_new))
            upd = _dot_tn(w["kd"], v_new.astype(BF16))
            state[d] = state[d] * w["s_decay"] + jnp.where(mask_bd, upd, 0.0)
    sf_scr[...] = state[0]
    sb_scr[...] = state[1]


def _dn_kernel(cq, cbg, lqf, lbgf, lqb, lbgb, ocf, ocb, olf, olb, sf_scr, sb_scr):
    step = pl.program_id(1)

    @pl.when(step == 0)
    def _():
        sf_scr[...] = jnp.zeros_like(sf_scr)
        sb_scr[...] = jnp.zeros_like(sb_scr)
        _dn_group(cq, cbg, cq, cbg, ocf, ocb, sf_scr, sb_scr)

    @pl.when(step > 0)
    def _():
        _dn_group(lqf, lbgf, lqb, lbgb, olf, olb, sf_scr, sb_scr)


def _scan_specs(width, ngroups):
    ctx = pl.BlockSpec((None, CTX_LEN, width), lambda b, s: (b, 0, 0))
    fwd = pl.BlockSpec((None, GROUP, width), lambda b, s: (b, jnp.maximum(s - 1, 0), 0))
    bwd = pl.BlockSpec((None, GROUP, width), lambda b, s: (b, ngroups - 1 - jnp.maximum(s - 1, 0), 0))
    return ctx, fwd, bwd


def _dn_scan(qkv_c, bg_c, qkv_l, bg_l):
    nb, seq = qkv_l.shape[0], qkv_l.shape[1]
    ng = seq // GROUP
    c768, f768, b768 = _scan_specs(768, ng)
    c128, f128, b128 = _scan_specs(LANES, ng)
    c256, f256, b256 = _scan_specs(256, ng)
    return pl.pallas_call(
        _dn_kernel,
        out_shape=(jax.ShapeDtypeStruct((nb, CTX_LEN, 256), F32), jax.ShapeDtypeStruct((nb, CTX_LEN, 256), F32),
                   jax.ShapeDtypeStruct((nb, seq, 256), F32), jax.ShapeDtypeStruct((nb, seq, 256), F32)),
        grid=(nb, ng + 1),
        in_specs=[c768, c128, f768, f128, b768, b128],
        out_specs=(c256, c256, f256, b256),
        scratch_shapes=[pltpu.VMEM((256, 256), F32), pltpu.VMEM((256, 256), F32)],
        compiler_params=_params(("parallel", "arbitrary")),
        name="dn_scan",
    )(qkv_c, bg_c, qkv_l, bg_l, qkv_l, bg_l)


def _gla_group(qk_f, v_f, m_f, qk_b, v_b, m_b, of_ref, ob_ref, sf_scr, sb_scr, w2, ba):
    t = qk_f.shape[0]
    n = t // CHUNK
    rows = _iota((CHUNK, 256), 0)
    lane_j = _iota((CHUNK, 256), 1) % CHUNK
    mask_k = _iota((256, 128), 0) // CHUNK == _iota((256, 128), 1) // GLA_DK
    mask_v = _iota((256, 256), 0) // CHUNK == _iota((256, 256), 1) // CHUNK
    mask_s = _iota((128, 256), 0) // GLA_DK == _iota((128, 256), 1) // CHUNK
    eye = (_iota((128, 128), 0) == _iota((128, 128), 1)).astype(F32)
    zero_b = jnp.zeros((), BF16)
    work = []
    for d, (qk, v, misc) in enumerate(((qk_f, v_f, m_f), (qk_b, v_b, m_b))):
        incl = (lane_j <= rows) if d == 0 else (lane_j >= rows)
        logit = _dot(misc[...], w2[:, d * 128:(d + 1) * 128], HI) + ba[:, d * 128:(d + 1) * 128]
        log_a = (jnp.minimum(logit, 0.0) - jnp.log1p(jnp.exp(-jnp.abs(logit)))) * (1.0 / GLA_TAU)
        rt = _iota((t, t), 0)
        ct = _iota((t, t), 1)
        cum = ((rt // CHUNK == ct // CHUNK) & ((ct <= rt) if d == 0 else (ct >= rt))).astype(F32)
        b_all = _mask_dot(cum, log_a)
        mid = CHUNK // 2 - 1 if d == 0 else CHUNK // 2
        last = CHUNK - 1 if d == 0 else 0
        for c in (range(n) if d == 0 else range(n - 1, -1, -1)):
            sl = slice(c * CHUNK, (c + 1) * CHUNK)
            q = qk[sl, 0:128].astype(F32) * (GLA_DK ** -0.5)
            k = qk[sl, 128:256].astype(F32)
            vb = v[sl, :]
            b = b_all[sl]
            b_mid = b[mid:mid + 1]
            b_last = b[last:last + 1]
            ke = jnp.where(mask_k, _tile4((k * jnp.exp(b_mid - b)).astype(BF16)), zero_b)
            a = jnp.where(incl, _dot_nt((q * jnp.exp(b - b_mid)).astype(BF16), ke), 0.0)
            o_intra = _dot(a.astype(BF16), jnp.where(mask_v, _tile4(vb), zero_b))
            ds = jnp.where(mask_s, _dot_tn((k * jnp.exp(b_last - b)).astype(BF16), vb), 0.0)
            dec_col = jnp.sum(eye * jnp.exp(b_last), axis=1, keepdims=True)
            work.append(dict(sl=sl, o_intra=o_intra, qb=(q * jnp.exp(b)).astype(BF16), ds=ds, dec_col=dec_col))
    state = [sf_scr[...], sb_scr[...]]
    out = (of_ref, ob_ref)
    for c in range(n):
        for d in range(2):
            w = work[d * n + c]
            out[d][w["sl"], :] = w["o_intra"] + _dot(w["qb"], state[d].astype(BF16))
            state[d] = w["dec_col"] * state[d] + w["ds"]
    sf_scr[...] = state[0]
    sb_scr[...] = state[1]


def _gla_kernel(cqk, cv, cm, lqkf, lvf, lmf, lqkb, lvb, lmb, w2_ref, ba_ref,
                ocf, ocb, olf, olb, sf_scr, sb_scr):
    step = pl.program_id(1)
    w2 = w2_ref[...]
    ba = ba_ref[...]

    @pl.when(step == 0)
    def _():
        sf_scr[...] = jnp.zeros_like(sf_scr)
        sb_scr[...] = jnp.zeros_like(sb_scr)
        _gla_group(cqk, cv, cm, cqk, cv, cm, ocf, ocb, sf_scr, sb_scr, w2, ba)

    @pl.when(step > 0)
    def _():
        _gla_group(lqkf, lvf, lmf, lqkb, lvb, lmb, olf, olb, sf_scr, sb_scr, w2, ba)


def _col_specs(width, col, ngroups):
    blk = col // width
    ctx = pl.BlockSpec((None, CTX_LEN, width), lambda b, s: (b, 0, blk))
    fwd = pl.BlockSpec((None, GROUP, width), lambda b, s: (b, jnp.maximum(s - 1, 0), blk))
    bwd = pl.BlockSpec((None, GROUP, width), lambda b, s: (b, ngroups - 1 - jnp.maximum(s - 1, 0), blk))
    return ctx, fwd, bwd


def _gla_scan(p_c, misc_c, p_l, misc_l, w2, ba):
    nb, seq = p_l.shape[0], p_l.shape[1]
    ng = seq // GROUP
    cqk, fqk, bqk = _col_specs(256, C_GQK, ng)
    cv, fv, bv = _col_specs(256, C_GV, ng)
    cm, fm, bm = _scan_specs(LANES, ng)
    c256, f256, b256 = _scan_specs(256, ng)
    const = lambda shape: pl.BlockSpec(shape, lambda b, s: (0, 0))
    return pl.pallas_call(
        _gla_kernel,
        out_shape=(jax.ShapeDtypeStruct((nb, CTX_LEN, 256), F32), jax.ShapeDtypeStruct((nb, CTX_LEN, 256), F32),
                   jax.ShapeDtypeStruct((nb, seq, 256), F32), jax.ShapeDtypeStruct((nb, seq, 256), F32)),
        grid=(nb, ng + 1),
        in_specs=[cqk, cv, cm, fqk, fv, fm, bqk, bv, bm, const((LANES, 256)), const((1, 256))],
        out_specs=(c256, c256, f256, b256),
        scratch_shapes=[pltpu.VMEM((128, 256), F32), pltpu.VMEM((128, 256), F32)],
        compiler_params=_params(("parallel", "arbitrary")),
        name="gla_scan",
    )(p_c, p_c, misc_c, p_l, p_l, misc_l, p_l, p_l, misc_l, w2, ba)


def _lane_group(width, group):
    return _iota((1, width), 1) // group


def _softmax_parts(parts):
    m = parts[0].max(axis=-1, keepdims=True)
    for s in parts[1:]:
        m = jnp.maximum(m, s.max(axis=-1, keepdims=True))
    ps = [jnp.exp2((s - m).astype(BF16)) for s in parts]
    total = ps[0].astype(F32).sum(axis=-1, keepdims=True)
    for p in ps[1:]:
        total = total + p.astype(F32).sum(axis=-1, keepdims=True)
    return ps, 1.0 / total


def _na_kernel(q_ref, kl_ref, vl_ref, kc_ref, vc_ref, bias_ref, o_ref, *, grid_rows):
    blk = pl.program_id(1)
    key_row0 = jnp.clip(blk * NA_QROWS - NA_WIN_ROWS // 2, 0, grid_rows - NA_KROWS)
    start = pl.multiple_of(key_row0 * GRID_W, GRID_W)
    kw = kl_ref[pl.ds(start, NA_KROWS * GRID_W), :]
    vw = vl_ref[pl.ds(start, NA_KROWS * GRID_W), :]
    kc = kc_ref[...]
    vc = vc_ref[...]
    q = q_ref[...]
    head = _lane_group(256, NA_DH)
    acc = jnp.zeros(q.shape, F32)
    for h in range(N_HEADS):
        qh = jnp.where(head == h, q, jnp.zeros_like(q))
        (p_w, p_c), inv = _softmax_parts([_dot_nt(qh, kw) + bias_ref[h], _dot_nt(qh, kc)])
        o_h = (_dot(p_w.astype(BF16), vw) + _dot(p_c.astype(BF16), vc)) * inv
        acc = acc + jnp.where(head == h, o_h, 0.0)
    o_ref[...] = acc.astype(BF16)


def _na_attention(qkv_l, qkv_c, bias):
    nb, seq = qkv_l.shape[0], qkv_l.shape[1]
    tq = NA_QROWS * GRID_W
    nblk = seq // tq
    variant = lambda j: jnp.where(j == 0, 0, jnp.where(j == nblk - 1, 2, 1))
    kern = functools.partial(_na_kernel, grid_rows=seq // GRID_W)
    return pl.pallas_call(
        kern,
        out_shape=jax.ShapeDtypeStruct((nb, seq, 256), BF16),
        grid=(nb, nblk),
        in_specs=[pl.BlockSpec((None, tq, 256), lambda b, j: (b, j, 0)),
                  pl.BlockSpec((None, seq, 256), lambda b, j: (b, 0, 1)),
                  pl.BlockSpec((None, seq, 256), lambda b, j: (b, 0, 2)),
                  pl.BlockSpec((None, CTX_LEN, 256), lambda b, j: (b, 0, 1)),
                  pl.BlockSpec((None, CTX_LEN, 256), lambda b, j: (b, 0, 2)),
                  pl.BlockSpec((None, N_HEADS, tq, NA_KROWS * GRID_W), lambda b, j: (variant(j), 0, 0, 0))],
        out_specs=pl.BlockSpec((None, tq, 256), lambda b, j: (b, j, 0)),
        compiler_params=_params(("parallel", "arbitrary")),
        name="na_attention",
    )(qkv_l, qkv_l, qkv_l, qkv_c, qkv_c, bias)


def _na_ctx_kernel(q_ref, k_ref, v_ref, o_ref):
    q = q_ref[...]
    k = k_ref[...]
    v = v_ref[...]
    head = _lane_group(256, NA_DH)
    acc = jnp.zeros(q.shape, F32)
    for h in range(N_HEADS):
        qh = jnp.where(head == h, q, jnp.zeros_like(q))
        (p,), inv = _softmax_parts([_dot_nt(qh, k)])
        acc = acc + jnp.where(head == h, _dot(p.astype(BF16), v) * inv, 0.0)
    o_ref[...] = acc.astype(BF16)


def _na_ctx_attention(qkv_c):
    nb = qkv_c.shape[0]
    spec = lambda c: pl.BlockSpec((None, CTX_LEN, 256), lambda b: (b, 0, c))
    return pl.pallas_call(
        _na_ctx_kernel,
        out_shape=jax.ShapeDtypeStruct((nb, CTX_LEN, 256), BF16),
        grid=(nb,),
        in_specs=[spec(0), spec(1), spec(2)],
        out_specs=spec(0),
        compiler_params=_params(("parallel",)),
        name="na_ctx_attention",
    )(qkv_c, qkv_c, qkv_c)


def _na_bias_tables(rpb, grid_rows):
    a = np.arange(NA_QROWS)[:, None, None, None]
    qc = np.arange(GRID_W)[None, :, None, None]
    b = np.arange(NA_KROWS)[None, None, :, None]
    kc = np.arange(GRID_W)[None, None, None, :]
    shape = (NA_QROWS, GRID_W, NA_KROWS, GRID_W)
    half = NA_WIN_ROWS // 2
    n_r, n_c = 2 * NA_WIN_ROWS - 1, 2 * NA_WIN_COLS - 1
    cs = np.clip(qc - NA_WIN_COLS // 2, 0, GRID_W - NA_WIN_COLS)
    ci = np.clip(kc - qc + NA_WIN_COLS - 1, 0, n_c - 1)[0, :, 0, :]
    pick_c = (ci[..., None] == np.arange(n_c)).astype(np.float32)
    vis, pick_r = [], []
    for q_row0 in (0, half, grid_rows - NA_QROWS):
        k_row0 = int(np.clip(q_row0 - half, 0, grid_rows - NA_KROWS))
        qr = q_row0 + a
        kr = k_row0 + b
        rs = np.clip(qr - half, 0, grid_rows - NA_WIN_ROWS)
        vis.append(np.broadcast_to((kr >= rs) & (kr < rs + NA_WIN_ROWS) & (kc >= cs) & (kc < cs + NA_WIN_COLS), shape))
        ri = np.clip(kr - qr + NA_WIN_ROWS - 1, 0, n_r - 1)[:, 0, :, 0]
        pick_r.append((ri[..., None] == np.arange(n_r)).astype(np.float32))
    t = jnp.einsum('hrc,vabr,qkc->vhaqbk', rpb, jnp.asarray(np.stack(pick_r)), jnp.asarray(pick_c), precision=HI)
    t = jnp.where(jnp.asarray(np.stack(vis))[:, None], t * LOG2E, NEG)
    return t.reshape(3, N_HEADS, NA_QROWS * GRID_W, NA_KROWS * GRID_W)


DA_KEY_BLOCK = 256
DA_VROWS = 64 + 16


def _slab_reduce(x, op, slab=64):
    folded = op(x.reshape(x.shape[0] // slab, slab, x.shape[1]), axis=0)
    return op(folded, axis=0, keepdims=True)


def _da_kernel(*refs, lam_init, has_lat):
    if has_lat:
        q_ref, kl_ref, vtl_ref, kc_ref, vtc_ref, lam_ref, g_ref, o_ref, st_scr, p_scr = refs
    else:
        q_ref, kc_ref, vtc_ref, lam_ref, g_ref, o_ref = refs
    q = q_ref[...]
    tq = q.shape[0]
    lp = lam_ref[...]
    lam = (jnp.exp(jnp.sum(lp[0:1] * lp[1:2], axis=1, keepdims=True))
           - jnp.exp(jnp.sum(lp[2:3] * lp[3:4], axis=1, keepdims=True)) + lam_init)
    qmap = _lane_group(256, DIFF_DQK)
    zero = jnp.zeros_like(q)
    heads = range(N_HEADS)
    hs = [slice(h * 64, (h + 1) * 64) for h in heads]
    vrows = [slice(h * DA_VROWS, (h + 1) * DA_VROWS) for h in heads]
    qs = [jnp.concatenate([jnp.where(qmap == 2 * h, q, zero), jnp.where(qmap == 2 * h + 1, q, zero)], axis=0)
          for h in heads]
    kc = kc_ref[...]
    st = [_dot_nt(kc, qs[h]) for h in heads]
    m = [_slab_reduce(st[h], jnp.max) for h in heads]
    acc = [_dot(vtc_ref[vrows[h], :], jnp.exp2((st[h] - m[h]).astype(BF16))) for h in heads]
    if has_lat:
        n_blocks = kl_ref.shape[0] // DA_KEY_BLOCK

        def block(i):
            start = i * DA_KEY_BLOCK
            return pl.ds(start if isinstance(i, int) else pl.multiple_of(start, DA_KEY_BLOCK), DA_KEY_BLOCK)

        def scores(i, slot):
            kb = kl_ref[block(i), :]
            for h in heads:
                st_scr[slot, h] = _dot_nt(kb, qs[h])

        def softmax_stage(slot, m):
            m_new = [jnp.maximum(m[h], _slab_reduce(st_scr[slot, h], jnp.max)) for h in heads]
            alpha = [jnp.exp2(m[h] - m_new[h]) for h in heads]
            for h in heads:
                p_scr[slot, h] = jnp.exp2((st_scr[slot, h] - m_new[h]).astype(BF16))
            return m_new, alpha

        def value_stage(acc, alpha, slot, i):
            return [alpha[h] * acc[h] + _dot(vtl_ref[vrows[h], block(i)], p_scr[slot, h]) for h in heads]

        scores(0, 0)
        alpha = None
        for i in range(n_blocks):
            slot = i % 2
            if i + 1 < n_blocks:
                scores(i + 1, 1 - slot)
            m, alpha_i = softmax_stage(slot, m)
            if i > 0:
                acc = value_stage(acc, alpha, 1 - slot, i - 1)
            alpha = alpha_i
        acc = value_stage(acc, alpha, (n_blocks - 1) % 2, n_blocks - 1)
    outs = []
    for h in heads:
        den = acc[h][64:65, :]
        o_h = acc[h][0:64, 0:tq] * (1.0 / den[:, 0:tq]) - acc[h][0:64, tq:2 * tq] * (lam / den[:, tq:2 * tq])
        y_h = o_h * lax.rsqrt(jnp.mean(o_h * o_h, axis=0, keepdims=True) + EPS) * g_ref[hs[h], :]
        outs.append(y_h * (1.0 - lam_init))
    o_ref[...] = jnp.concatenate(outs, axis=0).T.astype(BF16)


def _da_attention(qkv_q, qkv_l, vt_l, qkv_c, vt_c, lam_params, gain_col, lam_init, tq):
    nb, nq = qkv_q.shape[0], qkv_q.shape[1]
    has_lat = qkv_l is not None
    full = lambda arr, c: pl.BlockSpec((None, arr.shape[1], 256), lambda b, j: (b, 0, c))
    full_t = lambda arr: pl.BlockSpec((None, arr.shape[1], arr.shape[2]), lambda b, j: (b, 0, 0))
    in_specs = [pl.BlockSpec((None, tq, 256), lambda b, j: (b, j, 0))]
    args = [qkv_q]
    if has_lat:
        in_specs += [full(qkv_l, 1), full_t(vt_l)]
        args += [qkv_l, vt_l]
    in_specs += [full(qkv_c, 1), full_t(vt_c),
                 pl.BlockSpec((4, DIFF_DQK), lambda b, j: (0, 0)), pl.BlockSpec((256, 1), lambda b, j: (0, 0))]
    args += [qkv_c, vt_c, lam_params, gain_col]
    scratch = []
    if has_lat:
        assert qkv_l.shape[1] % DA_KEY_BLOCK == 0
        scratch = [pltpu.VMEM((2, N_HEADS, DA_KEY_BLOCK, 2 * tq), F32),
                   pltpu.VMEM((2, N_HEADS, DA_KEY_BLOCK, 2 * tq), BF16)]
    return pl.pallas_call(
        functools.partial(_da_kernel, lam_init=lam_init, has_lat=has_lat),
        out_shape=jax.ShapeDtypeStruct((nb, nq, 256), BF16),
        grid=(nb, nq // tq),
        in_specs=in_specs,
        out_specs=pl.BlockSpec((None, tq, 256), lambda b, j: (b, j, 0)),
        scratch_shapes=scratch,
        compiler_params=_params(("parallel", "arbitrary")),
        name="diff_attention",
    )(*args)


def _merge_kernel(x_ref, g1_ref, dnf_ref, dnb_ref, z_ref, glf_ref, glb_ref, r_ref, na_ref, df_ref, gate_ref,
                  bgate_ref, dng_ref, glg_ref, wb_ref, wo_ref, o_ref):
    def head_norm(o, gate, g):
        return ((o * lax.rsqrt(_group_mean(o * o, 64) + EPS) * g) * _silu(gate.astype(F32))).astype(BF16)

    ys = (head_norm(dnf_ref[...] + dnb_ref[...], z_ref[...], dng_ref[...]), na_ref[...],
          head_norm(glf_ref[...] + glb_ref[...], r_ref[...], glg_ref[...]), df_ref[...])
    acc = jnp.zeros(x_ref.shape, F32)
    for g in range(N_BRANCH):
        gate = _sigmoid(gate_ref[:, g * D_MODEL:(g + 1) * D_MODEL].astype(F32) + bgate_ref[g:g + 1, :])
        acc = acc + gate * _dot(ys[g], wb_ref[g])
    o_ref[...] = x_ref[...] + g1_ref[...] * _dot(acc.astype(BF16), wo_ref[...])


def _merge(x2, g1, dn_f, dn_b, gl_f, gl_b, y_na, y_df, p, b_gate, dn_g, gl_g, w_branch, w_out, li, rows_per_mod, tm):
    rows = x2.shape[0]
    row = lambda w, c=0: pl.BlockSpec((tm, w), lambda i: (i, c))
    const = lambda shape: pl.BlockSpec(shape, lambda i: (0,) * len(shape))
    layer = lambda shape: pl.BlockSpec((None,) + shape, lambda i: (li,) + (0,) * len(shape))
    return pl.pallas_call(
        _merge_kernel,
        out_shape=jax.ShapeDtypeStruct((rows, D_MODEL), F32),
        grid=(rows // tm,),
        in_specs=[row(D_MODEL),
                  pl.BlockSpec((None, 1, D_MODEL), lambda i: ((i * tm) // rows_per_mod, 0, 0)),
                  row(256), row(256), row(256, C_DNZ // 256),
                  row(256), row(256), row(256, C_GR // 256),
                  row(256), row(256), row(N_BRANCH * D_MODEL, C_GATE // (N_BRANCH * D_MODEL)),
                  const((N_BRANCH, D_MODEL)), const((1, 256)), const((1, 256)),
                  layer((N_BRANCH, BRANCH_W, D_MODEL)), layer((D_MODEL, D_MODEL))],
        out_specs=row(D_MODEL),
        compiler_params=_params(("parallel",)),
        name="merge",
    )(x2, g1, dn_f, dn_b, p, gl_f, gl_b, p, y_na, y_df, p, b_gate, dn_g, gl_g, w_branch, w_out)


FFN_COLS = 256


def _ffn_kernel(x_ref, xp_ref, xn_ref, gain_ref, sc_ref, sh_ref, g2_ref, wu_ref, wv_ref, cw_ref, cb_ref, wo_ref,
                o_ref, u_scr, act_scr, *, tm, seq_len):
    i = pl.program_id(0)
    first = (i * tm) % seq_len == 0
    last = ((i + 1) * tm) % seq_len == 0

    def norm(x):
        y = x * lax.rsqrt(jnp.mean(x * x, axis=-1, keepdims=True) + EPS)
        return (y * gain_ref[...]) * sc_ref[...] + sh_ref[...]

    x = x_ref[...]
    h = norm(x)
    he = jnp.concatenate([norm(xp_ref[...]), h, norm(xn_ref[...])], axis=0).astype(BF16)
    h = h.astype(BF16)
    rows = _iota((tm + 2 * SUBLANES, 1), 0)
    keep = jnp.logical_not((first & (rows < SUBLANES)) | (last & (rows >= tm + SUBLANES)))
    u_scr[...] = jnp.where(keep, _dot(he, wu_ref[...]), 0.0)
    for c in range(D_FF // FFN_COLS):
        cols = slice(c * FFN_COLS, (c + 1) * FFN_COLS)
        conv =(u_scr[SUBLANES - 1:SUBLANES - 1 + tm, cols] * cw_ref[0:1, cols]
                + u_scr[SUBLANES:SUBLANES + tm, cols] * cw_ref[1:2, cols]
                + u_scr[SUBLANES + 1:SUBLANES + 1 + tm, cols] * cw_ref[2:3, cols]) + cb_ref[:, cols]
        act_scr[:, cols] = (_silu(conv) * _dot(h, wv_ref[:, cols])).astype(BF16)
    o_ref[...] = x + g2_ref[...] * _dot(act_scr[...], wo_ref[...])


def _ffn(x2, gain, scale1p, shift, g2, w_u, w_v, conv_w, conv_b, w_o, li, rows_per_mod, seq_len, tm):
    rows = x2.shape[0]
    nb8 = rows // SUBLANES
    t8 = tm // SUBLANES
    const = lambda shape: pl.BlockSpec(shape, lambda i: (0,) * len(shape))
    weight = lambda shape: pl.BlockSpec((None,) + shape, lambda i: (li,) + (0,) * len(shape),
                                        pipeline_mode=pl.Buffered(1))
    mod = pl.BlockSpec((None, 1, D_MODEL), lambda i: ((i * tm) // rows_per_mod, 0, 0))
    return pl.pallas_call(
        functools.partial(_ffn_kernel, tm=tm, seq_len=seq_len),
        out_shape=jax.ShapeDtypeStruct((rows, D_MODEL), F32),
        grid=(rows // tm,),
        in_specs=[pl.BlockSpec((tm, D_MODEL), lambda i: (i, 0)),
                  pl.BlockSpec((SUBLANES, D_MODEL), lambda i: (jnp.maximum(i * t8 - 1, 0), 0)),
                  pl.BlockSpec((SUBLANES, D_MODEL), lambda i: (jnp.minimum((i + 1) * t8, nb8 - 1), 0)),
                  const((1, D_MODEL)), mod, mod, mod,
                  weight((D_MODEL, D_FF)), weight((D_MODEL, D_FF)), const((3, D_FF)), const((1, D_FF)),
                  weight((D_FF, D_MODEL))],
        out_specs=pl.BlockSpec((tm, D_MODEL), lambda i: (i, 0)),
        scratch_shapes=[pltpu.VMEM((tm + 2 * SUBLANES, D_FF), F32), pltpu.VMEM((tm, D_FF), BF16)],
        compiler_params=_params(("parallel",)),
        name="conv_ffn",
    )(x2, x2, x2, gain, scale1p, shift, g2, w_u, w_v, conv_w, conv_b, w_o)


def _rotate_half_perm():
    quarter = DIFF_DQK // 4
    return np.concatenate([np.arange(quarter, 2 * quarter), np.arange(0, quarter),
                           np.arange(3 * quarter, 4 * quarter), np.arange(2 * quarter, 3 * quarter)])


def _inproj_weights(w_in):
    w = w_in.astype(BF16)
    widths = (768, 256, 8, 8, 768, 128, 128, 256, 256, 32, 768, 4096)
    o = np.concatenate([[0], np.cumsum(widths)])
    seg = lambda a, b: w[:, :, a:b]
    zeros = lambda n: jnp.zeros(w.shape[:2] + (n,), BF16)

    def rot(t):
        t6 = t.reshape(t.shape[:2] + (256 // DIFF_DQK, 2, 2, DIFF_DQK // 4))
        sign = jnp.asarray([-1.0, 1.0], BF16).reshape(1, 1, 1, 1, 2, 1)
        return (jnp.flip(t6, axis=4) * sign).reshape(t.shape)

    pieces = [seg(o[0], o[1]), seg(o[4], o[5]), seg(o[10], o[11]), seg(o[1], o[2]),
              rot(seg(o[10], o[10] + 256)), rot(seg(o[10] + 256, o[10] + 512)),
              seg(o[5], o[7]), seg(o[7], o[8]), seg(o[8], o[9]),
              seg(o[2], o[4]), seg(o[9], o[10]), zeros(C_GATE - C_MISC - 48), seg(o[11], o[12])]
    return jnp.concatenate(pieces, axis=-1)


def _rope_tables(length):
    t = jnp.arange(length)
    row = (t // GRID_W).astype(F32)
    col = (t % GRID_W).astype(F32)
    n_freq = DIFF_DQK // 4
    inv_freq = jnp.power(jnp.float32(ROPE_THETA), -jnp.arange(n_freq, dtype=F32) / n_freq)
    ang_r = row[:, None] * inv_freq
    ang_c = col[:, None] * inv_freq
    ang = jnp.concatenate([ang_r, ang_r, ang_c, ang_c], axis=-1)
    reps = 256 // DIFF_DQK
    return jnp.tile(jnp.cos(ang), (1, reps)), jnp.tile(jnp.sin(ang), (1, reps))


def _pad_lanes(v, offset, width):
    return jnp.zeros((1, width), F32).at[0, offset:offset + v.shape[0]].set(v)


def kernel(x, c, ctx, c_ctx, w_mod, b_mod, norm1_g, norm2_g, w_in, b_gate, dn_conv, dn_a_log, dn_dt_bias, dn_norm_g,
           na_q_norm, na_k_norm, na_rpb, gla_w_a2, gla_b_a, gla_norm_g, df_q_norm, df_k_norm, df_lambda, df_norm_g,
           w_branch, w_out, ffn_w_in, ffn_conv_w, ffn_conv_b, ffn_w_out):
    nb, seq, _ = x.shape
    n_lat, n_ctx = nb * seq, nb * CTX_LEN
    tm_lat, tm_ctx = ROWS_LAT, ROWS_CTX

    cc = jnp.zeros((16, D_MODEL), F32).at[0:nb].set(c).at[nb].set(c_ctx)
    mod = _modulation(cc, w_mod, b_mod)

    perm = _rotate_half_perm()
    w_in_p = _inproj_weights(w_in)
    w_branch_b = w_branch.astype(BF16)
    w_out_b = w_out.astype(BF16)
    w_u = ffn_w_in[:, :, 0:D_FF].astype(BF16)
    w_v = ffn_w_in[:, :, D_FF:2 * D_FF].astype(BF16)
    w_o = ffn_w_out.astype(BF16)
    cos, sin = _rope_tables(seq)

    x2 = x.reshape(n_lat, D_MODEL)
    xc2 = ctx.reshape(n_ctx, D_MODEL)
    for li in range(DEPTH):
        with_ctx = li < DEPTH - 1
        lam_init = 0.8 - 0.6 * math.exp(-0.3 * li)
        m = mod[li]
        part = lambda k: m[:, k * D_MODEL:(k + 1) * D_MODEL]
        lat = lambda v: v[0:nb].reshape(nb, 1, D_MODEL)
        cx = lambda v: v[nb:nb + 1].reshape(1, 1, D_MODEL)
        sh1, sc1, g1, sh2, sc2, g2 = (part(k) for k in range(6))
        n1 = norm1_g[li].reshape(1, D_MODEL)
        n2 = norm2_g[li].reshape(1, D_MODEL)

        p_l, misc_l = _inproj(x2, n1, lat(1.0 + sc1), lat(sh1), w_in_p, li, seq, INPROJ_ROWS)
        p_c, misc_c = _inproj(xc2, n1, cx(1.0 + sc1), cx(sh1), w_in_p, li, n_ctx, INPROJ_ROWS)

        tile4 = lambda v: jnp.tile(v, 256 // v.shape[0]).reshape(1, 256)
        vecs = (dn_conv[li], _pad_lanes(dn_a_log[li].reshape(-1), 2 * N_HEADS, LANES),
                _pad_lanes(dn_dt_bias[li].reshape(-1), 2 * N_HEADS, LANES),
                tile4(na_q_norm[li]), tile4(na_k_norm[li]), tile4(df_q_norm[li]), tile4(df_k_norm[li]),
                tile4(df_q_norm[li][perm]), tile4(df_k_norm[li][perm]))
        dn_l, bg_l, na_l, df_l, vt_l = _prep(p_l, misc_l, cos, sin, vecs, tp=tm_lat, seq_len=seq, rope=True)
        dn_c, bg_c, na_c, df_c, vt_c = _prep(p_c, misc_c, cos, sin, vecs, tp=tm_ctx, seq_len=CTX_LEN, rope=False)

        b3 = lambda a, t: a.reshape(nb, t, a.shape[-1])
        dn_cf, dn_cb, dn_lf, dn_lb = _dn_scan(b3(dn_c, CTX_LEN), b3(bg_c, CTX_LEN), b3(dn_l, seq), b3(bg_l, seq))

        w2 = jnp.zeros((LANES, 256), F32)
        a1_off = 4 * N_HEADS
        w2 = w2.at[a1_off:a1_off + GLA_RANK, 0:128].set(gla_w_a2[li, 0])
        w2 = w2.at[a1_off + GLA_RANK:a1_off + 2 * GLA_RANK, 128:256].set(gla_w_a2[li, 1])
        gl_cf, gl_cb, gl_lf, gl_lb = _gla_scan(b3(p_c, CTX_LEN), b3(misc_c, CTX_LEN), b3(p_l, seq), b3(misc_l, seq),
                                               w2, gla_b_a[li].reshape(1, 256))

        y_na = _na_attention(b3(na_l, seq), b3(na_c, CTX_LEN), _na_bias_tables(na_rpb[li], seq // GRID_W))
        df_gain = tile4(df_norm_g[li]).reshape(256, 1)
        y_df = _da_attention(b3(df_l, seq), b3(df_l, seq), vt_l, b3(df_c, CTX_LEN), vt_c, df_lambda[li],
                             df_gain, lam_init, DA_QUERIES_LAT)

        flat = lambda a: a.reshape(-1, a.shape[-1])
        gains = (b_gate[li], tile4(dn_norm_g[li]), tile4(gla_norm_g[li]), w_branch_b, w_out_b, li)
        ffn_w = (w_u, w_v, ffn_conv_w[li], ffn_conv_b[li].reshape(1, D_FF), w_o, li)
        x2 = _merge(x2, lat(g1), flat(dn_lf), flat(dn_lb), flat(gl_lf), flat(gl_lb), flat(y_na), flat(y_df), p_l,
                    *gains, seq, tm_lat)
        x2 = _ffn(x2, n2, lat(1.0 + sc2), lat(sh2), lat(g2), *ffn_w, seq, seq, tm_lat)
        if with_ctx:
            yc_na = _na_ctx_attention(b3(na_c, CTX_LEN))
            yc_df = _da_attention(b3(df_c, CTX_LEN), None, None, b3(df_c, CTX_LEN), vt_c, df_lambda[li],
                                  df_gain, lam_init, DA_QUERIES_CTX)
            xc2 = _merge(xc2, cx(g1), flat(dn_cf), flat(dn_cb), flat(gl_cf), flat(gl_cb), flat(yc_na), flat(yc_df),
                         p_c, *gains, n_ctx, tm_ctx)
            xc2 = _ffn(xc2, n2, cx(1.0 + sc2), cx(sh2), cx(g2), *ffn_w, n_ctx, CTX_LEN, tm_ctx)
    return x2.reshape(nb, seq, D_MODEL)
```

```python
import functools
import math

import numpy as np
import jax
import jax.numpy as jnp
from jax import lax
from jax.experimental import pallas as pl
from jax.experimental.pallas import tpu as pltpu

F32 = jnp.float32
BF16 = jnp.bfloat16
HI = lax.Precision.HIGHEST

D_MODEL = 1024
DEPTH = 4
GRID_W = 64
CTX_LEN = 256
N_BRANCH = 4
BRANCH_W = 256
CHUNK = 64
N_HEADS = 4
DN_DK = 64
DN_CONV = 3
NA_DH = 64
NA_WIN_ROWS = 8
NA_WIN_COLS = 16
GLA_DK = 32
GLA_RANK = 16
GLA_TAU = 16.0
DIFF_DQK = 32
ROPE_THETA = 10000.0
D_FF = 2816
EPS = 1e-6
NEG = -1e30
LOG2E = 1.4426950408889634

LANES = 128
SUBLANES = 8
HALO = 16
VMEM_LIMIT = 56 * 1024 * 1024

C_DN = 0
C_NA = 768
C_DF = 1536
C_DNZ = 2304
C_DFROT = 2560
C_GQK = 3072
C_GV = 3328
C_GR = 3584
C_MISC = 3840
C_GATE = 4096
P_WIDTH = 8192

GROUP = 4 * CHUNK
NA_QROWS = 4
NA_KROWS = 12

ROWS_LAT = 512
ROWS_CTX = CTX_LEN
INPROJ_ROWS = 2048
INPROJ_COLS = 2048
DA_QUERIES_LAT = 512
DA_QUERIES_CTX = 128


def _dot(a, b, prec=None):
    return jnp.dot(a, b, preferred_element_type=F32, precision=prec)


def _dot_nt(a, b, prec=None):
    return lax.dot_general(a, b, (((1,), (1,)), ((), ())), preferred_element_type=F32, precision=prec)


def _dot_tn(a, b, prec=None):
    return lax.dot_general(a, b, (((0,), (0,)), ((), ())), preferred_element_type=F32, precision=prec)


def _iota(shape, dim):
    return lax.broadcasted_iota(jnp.int32, shape, dim)


def _group_mask(rows, cols, rgroup, cgroup):
    return (_iota((rows, cols), 0) // rgroup == _iota((rows, cols), 1) // cgroup).astype(F32)


def _group_mean(xsq, width):
    n = xsq.shape[-1]
    ones = _group_mask(n, n, width, width).astype(BF16)
    return _dot_mask(xsq, ones) * (1.0 / width)


def _split(x, terms):
    parts = []
    for _ in range(terms):
        piece = x.astype(BF16)
        parts.append(piece)
        x = x - piece.astype(F32)
    return parts


def _dot_mask(x, mask01, terms=2):
    m = mask01.astype(BF16)
    return sum(_dot(piece, m) for piece in _split(x, terms))


def _mask_dot(mask01, x, terms=3):
    m = mask01.astype(BF16)
    return sum(_dot(m, piece) for piece in _split(x, terms))


def _sigmoid(x):
    return 0.5 * jnp.tanh(0.5 * x) + 0.5


def _silu(x):
    return x * _sigmoid(x)


def _softplus(x):
    return jnp.maximum(x, 0.0) + jnp.log1p(jnp.exp(-jnp.abs(x)))


def _params(sem):
    return pltpu.CompilerParams(dimension_semantics=sem, vmem_limit_bytes=VMEM_LIMIT)


def _mod_kernel(c_ref, w_ref, b_ref, o_ref):
    o_ref[...] = _dot(_silu(c_ref[...]), w_ref[...], HI) + b_ref[...]


def _modulation(cc, w_mod, b_mod):
    tn = 1024
    return pl.pallas_call(
        _mod_kernel,
        out_shape=jax.ShapeDtypeStruct((DEPTH, 16, 6 * D_MODEL), F32),
        grid=(DEPTH, 6 * D_MODEL // tn),
        in_specs=[pl.BlockSpec((16, D_MODEL), lambda l, j: (0, 0)),
                  pl.BlockSpec((None, D_MODEL, tn), lambda l, j: (l, 0, j)),
                  pl.BlockSpec((None, 1, tn), lambda l, j: (l, 0, j))],
        out_specs=pl.BlockSpec((None, 16, tn), lambda l, j: (l, 0, j)),
        compiler_params=_params(("arbitrary", "arbitrary")),
        name="modulation",
    )(cc, w_mod, b_mod.reshape(DEPTH, 1, 6 * D_MODEL))


def _inproj_kernel(x_ref, g_ref, sc_ref, sh_ref, w_ref, o_ref, misc_ref, h_scr, *, tn):
    j = pl.program_id(1)

    @pl.when(j == 0)
    def _():
        x = x_ref[...]
        y = x * lax.rsqrt(jnp.mean(x * x, axis=-1, keepdims=True) + EPS)
        h_scr[...] = ((y * g_ref[...]) * sc_ref[...] + sh_ref[...]).astype(BF16)

    acc = _dot(h_scr[...], w_ref[...])
    o_ref[...] = acc.astype(BF16)

    @pl.when(j == C_MISC // tn)
    def _():
        misc_ref[...] = acc[:, C_MISC % tn:C_MISC % tn + LANES]


def _inproj(x2, gain, scale1p, shift, w, li, rows_per_mod, tm):
    rows = x2.shape[0]
    tn = INPROJ_COLS
    mod_spec = pl.BlockSpec((None, 1, D_MODEL), lambda i, j: ((i * tm) // rows_per_mod, 0, 0))
    return pl.pallas_call(
        functools.partial(_inproj_kernel, tn=tn),
        out_shape=(jax.ShapeDtypeStruct((rows, P_WIDTH), BF16), jax.ShapeDtypeStruct((rows, LANES), F32)),
        grid=(rows // tm, P_WIDTH // tn),
        in_specs=[pl.BlockSpec((tm, D_MODEL), lambda i, j: (i, 0)),
                  pl.BlockSpec((1, D_MODEL), lambda i, j: (0, 0)),
                  mod_spec, mod_spec,
                  pl.BlockSpec((None, D_MODEL, tn), lambda i, j: (li, 0, j))],
        out_specs=(pl.BlockSpec((tm, tn), lambda i, j: (i, j)), pl.BlockSpec((tm, LANES), lambda i, j: (i, 0))),
        scratch_shapes=[pltpu.VMEM((tm, D_MODEL), BF16)],
        compiler_params=_params(("parallel", "arbitrary")),
        name="inproj",
    )(x2, gain, scale1p, shift, w)


def _prep_kernel(dn_ref, dnp_ref, dnn_ref, misc_ref, na_ref, df_ref, dfrot_ref, cos_ref, sin_ref,
                 convw_ref, alog_ref, dt_ref, naq_ref, nak_ref, dfq_ref, dfk_ref, dfqr_ref, dfkr_ref,
                 dn_o, bg_o, na_o, df_o, vt_o, xe_scr, *, tp, seq_len, rope):
    i = pl.program_id(0)
    first = (i * tp) % seq_len == 0
    last = ((i + 1) * tp) % seq_len == 0
    halo_prev = dnp_ref[...].astype(F32)[HALO - SUBLANES:HALO]
    halo_next = dnn_ref[...].astype(F32)[0:SUBLANES]
    xe_scr[0:SUBLANES, :] = jnp.where(first, 0.0, halo_prev)
    xe_scr[SUBLANES:SUBLANES + tp, :] = dn_ref[...].astype(F32)
    xe_scr[SUBLANES + tp:2 * SUBLANES + tp, :] = jnp.where(last, 0.0, halo_next)
    cw = convw_ref[...]
    conv = (xe_scr[SUBLANES - 1:SUBLANES - 1 + tp, :] * cw[0:1]
            + xe_scr[SUBLANES:SUBLANES + tp, :] * cw[1:2]
            + xe_scr[SUBLANES + 1:SUBLANES + 1 + tp, :] * cw[2:3])
    s = _silu(conv)
    q, k, v = s[:, 0:256], s[:, 256:512], s[:, 512:768]
    qn = q * lax.rsqrt(_group_mean(q * q, DN_DK) * DN_DK + EPS) * (DN_DK ** -0.5)
    kn = k * lax.rsqrt(_group_mean(k * k, DN_DK) * DN_DK + EPS)
    dn_o[:, 0:256] = qn
    dn_o[:, 256:512] = kn
    dn_o[:, 512:768] = v
    m = misc_ref[...]
    beta = _sigmoid(m)
    g = -jnp.exp(alog_ref[...]) * _softplus(m + dt_ref[...])
    bg_o[...] = jnp.where(_iota(m.shape, 1) < 2 * N_HEADS, beta, g)
    q, k = na_ref[:, 0:256].astype(F32), na_ref[:, 256:512].astype(F32)
    qn = q * lax.rsqrt(_group_mean(q * q, NA_DH) + EPS) * naq_ref[...] * (NA_DH ** -0.5 * LOG2E)
    kn = k * lax.rsqrt(_group_mean(k * k, NA_DH) + EPS) * nak_ref[...]
    na_o[:, 0:256] = qn.astype(BF16)
    na_o[:, 256:512] = kn.astype(BF16)
    na_o[:, 512:768] = na_ref[:, 512:768]
    q, k = df_ref[:, 0:256].astype(F32), df_ref[:, 256:512].astype(F32)
    rq = lax.rsqrt(_group_mean(q * q, DIFF_DQK) + EPS)
    rk = lax.rsqrt(_group_mean(k * k, DIFF_DQK) + EPS)
    qn = q * rq * dfq_ref[...]
    kn = k * rk * dfk_ref[...]
    if rope:
        xr = dfrot_ref[...].astype(F32)
        cos, sin = cos_ref[...], sin_ref[...]
        qn = qn * cos + (xr[:, 0:256] * rq * dfqr_ref[...]) * sin
        kn = kn * cos + (xr[:, 256:512] * rk * dfkr_ref[...]) * sin
    df_o[:, 0:256] = (qn * (DIFF_DQK ** -0.5 * LOG2E)).astype(BF16)
    df_o[:, 256:512] = kn.astype(BF16)
    df_o[:, 512:768] = df_ref[:, 512:768]
    vt = df_ref[:, 512:768].astype(F32).T
    ones = jnp.ones((DA_VROWS - 64, tp), F32)
    vt_o[...] = jnp.concatenate([piece for h in range(N_HEADS) for piece in (vt[h * 64:(h + 1) * 64], ones)],
                                axis=0).astype(BF16)


def _prep(p, misc, cos, sin, vecs, *, tp, seq_len, rope):
    rows = p.shape[0]
    nb8 = rows // HALO
    t8 = tp // HALO
    nseq_tiles = max(seq_len // tp, 1)
    row_vec = lambda w: pl.BlockSpec((1, w), lambda i: (0, 0))
    kern = functools.partial(_prep_kernel, tp=tp, seq_len=seq_len, rope=rope)
    return pl.pallas_call(
        kern,
        out_shape=(jax.ShapeDtypeStruct((rows, 768), F32), jax.ShapeDtypeStruct((rows, LANES), F32),
                   jax.ShapeDtypeStruct((rows, 768), BF16), jax.ShapeDtypeStruct((rows, 768), BF16),
                   jax.ShapeDtypeStruct((rows // seq_len, N_HEADS * DA_VROWS, seq_len), BF16)),
        grid=(rows // tp,),
        in_specs=[pl.BlockSpec((tp, 768), lambda i: (i, C_DN // 768)),
                  pl.BlockSpec((HALO, 768), lambda i: (jnp.maximum(i * t8 - 1, 0), C_DN // 768)),
                  pl.BlockSpec((HALO, 768), lambda i: (jnp.minimum((i + 1) * t8, nb8 - 1), C_DN // 768)),
                  pl.BlockSpec((tp, LANES), lambda i: (i, 0)),
                  pl.BlockSpec((tp, 768), lambda i: (i, C_NA // 768)),
                  pl.BlockSpec((tp, 768), lambda i: (i, C_DF // 768)),
                  pl.BlockSpec((tp, 512), lambda i: (i, C_DFROT // 512)),
                  pl.BlockSpec((tp, 256), lambda i: (i % nseq_tiles, 0)),
                  pl.BlockSpec((tp, 256), lambda i: (i % nseq_tiles, 0)),
                  pl.BlockSpec((DN_CONV, 768), lambda i: (0, 0)),
                  row_vec(LANES), row_vec(LANES),
                  row_vec(256), row_vec(256), row_vec(256), row_vec(256), row_vec(256), row_vec(256)],
        out_specs=(pl.BlockSpec((tp, 768), lambda i: (i, 0)), pl.BlockSpec((tp, LANES), lambda i: (i, 0)),
                   pl.BlockSpec((tp, 768), lambda i: (i, 0)), pl.BlockSpec((tp, 768), lambda i: (i, 0)),
                   pl.BlockSpec((None, N_HEADS * DA_VROWS, tp), lambda i: (i // nseq_tiles, 0, i % nseq_tiles))),
        scratch_shapes=[pltpu.VMEM((tp + 2 * SUBLANES, 768), F32)],
        compiler_params=_params(("parallel",)),
        name="prep",
    )(p, p, p, misc, p, p, p, cos, sin, *vecs)


def _tile4(x):
    return jnp.concatenate([x, x, x, x], axis=0)


def _dn_group(qkv_f, bg_f, qkv_b, bg_b, of_ref, ob_ref, sf_scr, sb_scr):
    t = qkv_f.shape[0]
    n = t // CHUNK
    rows = _iota((CHUNK, 256), 0)
    lane_j = _iota((CHUNK, 256), 1) % CHUNK
    eye_cat = (lane_j == rows).astype(F32)
    mask_bd = _iota((256, 256), 0) // CHUNK == _iota((256, 256), 1) // CHUNK
    zero_b = jnp.zeros((), BF16)

    def bd(x):
        return jnp.where(mask_bd, _tile4(x.astype(BF16)), zero_b)

    work = []
    for d, (qkv, bg) in enumerate(((qkv_f, bg_f), (qkv_b, bg_b))):
        incl = (lane_j <= rows) if d == 0 else (lane_j >= rows)
        strict = (lane_j < rows) if d == 0 else (lane_j > rows)
        srow = _iota((LANES, 512), 0)
        scol = _iota((LANES, 512), 1)
        sel = (srow == (scol // 256) * 2 * N_HEADS + d * N_HEADS + (scol % 256) // CHUNK).astype(F32)
        e = _dot_mask(bg[...], sel, terms=3)
        rt = _iota((t, t), 0)
        ct = _iota((t, t), 1)
        cum = ((rt // CHUNK == ct // CHUNK) & ((ct <= rt) if d == 0 else (ct >= rt))).astype(F32)
        gc_all = _mask_dot(cum, e[:, 256:512])
        for c in (range(n) if d == 0 else range(n - 1, -1, -1)):
            sl = slice(c * CHUNK, (c + 1) * CHUNK)
            q, k, v = qkv[sl, 0:256], qkv[sl, 256:512], qkv[sl, 512:768]
            beta, gc = e[sl, 0:256], gc_all[sl]
            gc_row = jnp.sum(eye_cat * gc, axis=0, keepdims=True)
            decay = jnp.exp(jnp.where(incl, gc - gc_row, -jnp.inf))
            gc_last = gc[CHUNK - 1:CHUNK] if d == 0 else gc[0:1]
            kb = k * beta
            gram = _dot_nt(jnp.concatenate([kb, q], axis=0).astype(BF16), bd(k))
            n_cat = jnp.where(strict, gram[0:CHUNK] * decay, 0.0)
            egc = jnp.exp(gc)
            work.append(dict(
                d=d, sl=sl, n_cat=n_cat, a_cat=jnp.where(incl, gram[CHUNK:2 * CHUNK] * decay, 0.0).astype(BF16),
                rhs_u=bd(v * beta), rhs_w=bd(kb * egc), qd=(q * egc).astype(BF16),
                kd=(k * jnp.exp(gc_last - gc)).astype(BF16), s_decay=jnp.exp(gc_last),
                p=eye_cat - jnp.where(rows // 2 == lane_j // 2, n_cat, 0.0)))
    size = 2
    while size < CHUNK:
        off = (rows // (2 * size) == lane_j // (2 * size)) & (rows // size != lane_j // size)
        half = [_dot(w["p"].astype(BF16), bd(jnp.where(off, w["n_cat"], 0.0))) for w in work]
        for w, hx in zip(work, half):
            w["p"] = w["p"] - _dot(hx.astype(BF16), bd(w["p"]))
        size *= 2
    for w in work:
        pb = w["p"].astype(BF16)
        w["u"] = _dot(pb, w["rhs_u"])
        w["wq"] = jnp.concatenate([_dot(pb, w["rhs_w"]).astype(BF16), w["qd"]], axis=0)
    state = [sf_scr[...], sb_scr[...]]
    out = (of_ref, ob_ref)
    for c in range(n):
        for d in range(2):
            w = work[d * n + c]
            ws = _dot(w["wq"], state[d].astype(BF16))
            v_new = w["u"] - ws[0:CHUNK]
            out[d][w["sl"], :] = ws[CHUNK:2 * CHUNK] + _dot(w["a_cat"], bd(v_new))
            upd = _dot_tn(w["kd"], v_new.astype(BF16))
            state[d] = state[d] * w["s_decay"] + jnp.where(mask_bd, upd, 0.0)
    sf_scr[...] = state[0]
    sb_scr[...] = state[1]


def _dn_kernel(cq, cbg, lqf, lbgf, lqb, lbgb, ocf, ocb, olf, olb, sf_scr, sb_scr):
    step = pl.program_id(1)

    @pl.when(step == 0)
    def _():
        sf_scr[...] = jnp.zeros_like(sf_scr)
        sb_scr[...] = jnp.zeros_like(sb_scr)
        _dn_group(cq, cbg, cq, cbg, ocf, ocb, sf_scr, sb_scr)

    @pl.when(step > 0)
    def _():
        _dn_group(lqf, lbgf, lqb, lbgb, olf, olb, sf_scr, sb_scr)


def _scan_specs(width, ngroups):
    ctx = pl.BlockSpec((None, CTX_LEN, width), lambda b, s: (b, 0, 0))
    fwd = pl.BlockSpec((None, GROUP, width), lambda b, s: (b, jnp.maximum(s - 1, 0), 0))
    bwd = pl.BlockSpec((None, GROUP, width), lambda b, s: (b, ngroups - 1 - jnp.maximum(s - 1, 0), 0))
    return ctx, fwd, bwd


def _dn_scan(qkv_c, bg_c, qkv_l, bg_l):
    nb, seq = qkv_l.shape[0], qkv_l.shape[1]
    ng = seq // GROUP
    c768, f768, b768 = _scan_specs(768, ng)
    c128, f128, b128 = _scan_specs(LANES, ng)
    c256, f256, b256 = _scan_specs(256, ng)
    return pl.pallas_call(
        _dn_kernel,
        out_shape=(jax.ShapeDtypeStruct((nb, CTX_LEN, 256), F32), jax.ShapeDtypeStruct((nb, CTX_LEN, 256), F32),
                   jax.ShapeDtypeStruct((nb, seq, 256), F32), jax.ShapeDtypeStruct((nb, seq, 256), F32)),
        grid=(nb, ng + 1),
        in_specs=[c768, c128, f768, f128, b768, b128],
        out_specs=(c256, c256, f256, b256),
        scratch_shapes=[pltpu.VMEM((256, 256), F32), pltpu.VMEM((256, 256), F32)],
        compiler_params=_params(("parallel", "arbitrary")),
        name="dn_scan",
    )(qkv_c, bg_c, qkv_l, bg_l, qkv_l, bg_l)


def _gla_group(qk_f, v_f, m_f, qk_b, v_b, m_b, of_ref, ob_ref, sf_scr, sb_scr, w2, ba):
    t = qk_f.shape[0]
    n = t // CHUNK
    rows = _iota((CHUNK, 256), 0)
    lane_j = _iota((CHUNK, 256), 1) % CHUNK
    mask_k = _iota((256, 128), 0) // CHUNK == _iota((256, 128), 1) // GLA_DK
    mask_v = _iota((256, 256), 0) // CHUNK == _iota((256, 256), 1) // CHUNK
    mask_s = _iota((128, 256), 0) // GLA_DK == _iota((128, 256), 1) // CHUNK
    eye = (_iota((128, 128), 0) == _iota((128, 128), 1)).astype(F32)
    zero_b = jnp.zeros((), BF16)
    work = []
    for d, (qk, v, misc) in enumerate(((qk_f, v_f, m_f), (qk_b, v_b, m_b))):
        incl = (lane_j <= rows) if d == 0 else (lane_j >= rows)
        logit = _dot(misc[...], w2[:, d * 128:(d + 1) * 128], HI) + ba[:, d * 128:(d + 1) * 128]
        log_a = (jnp.minimum(logit, 0.0) - jnp.log1p(jnp.exp(-jnp.abs(logit)))) * (1.0 / GLA_TAU)
        rt = _iota((t, t), 0)
        ct = _iota((t, t), 1)
        cum = ((rt // CHUNK == ct // CHUNK) & ((ct <= rt) if d == 0 else (ct >= rt))).astype(F32)
        b_all = _mask_dot(cum, log_a)
        mid = CHUNK // 2 - 1 if d == 0 else CHUNK // 2
        last = CHUNK - 1 if d == 0 else 0
        for c in (range(n) if d == 0 else range(n - 1, -1, -1)):
            sl = slice(c * CHUNK, (c + 1) * CHUNK)
            q = qk[sl, 0:128].astype(F32) * (GLA_DK ** -0.5)
            k = qk[sl, 128:256].astype(F32)
            vb = v[sl, :]
            b = b_all[sl]
            b_mid = b[mid:mid + 1]
            b_last = b[last:last + 1]
            ke = jnp.where(mask_k, _tile4((k * jnp.exp(b_mid - b)).astype(BF16)), zero_b)
            a = jnp.where(incl, _dot_nt((q * jnp.exp(b - b_mid)).astype(BF16), ke), 0.0)
            o_intra = _dot(a.astype(BF16), jnp.where(mask_v, _tile4(vb), zero_b))
            ds = jnp.where(mask_s, _dot_tn((k * jnp.exp(b_last - b)).astype(BF16), vb), 0.0)
            dec_col = jnp.sum(eye * jnp.exp(b_last), axis=1, keepdims=True)
            work.append(dict(sl=sl, o_intra=o_intra, qb=(q * jnp.exp(b)).astype(BF16), ds=ds, dec_col=dec_col))
    state = [sf_scr[...], sb_scr[...]]
    out = (of_ref, ob_ref)
    for c in range(n):
        for d in range(2):
            w = work[d * n + c]
            out[d][w["sl"], :] = w["o_intra"] + _dot(w["qb"], state[d].astype(BF16))
            state[d] = w["dec_col"] * state[d] + w["ds"]
    sf_scr[...] = state[0]
    sb_scr[...] = state[1]


def _gla_kernel(cqk, cv, cm, lqkf, lvf, lmf, lqkb, lvb, lmb, w2_ref, ba_ref,
                ocf, ocb, olf, olb, sf_scr, sb_scr):
    step = pl.program_id(1)
    w2 = w2_ref[...]
    ba = ba_ref[...]

    @pl.when(step == 0)
    def _():
        sf_scr[...] = jnp.zeros_like(sf_scr)
        sb_scr[...] = jnp.zeros_like(sb_scr)
        _gla_group(cqk, cv, cm, cqk, cv, cm, ocf, ocb, sf_scr, sb_scr, w2, ba)

    @pl.when(step > 0)
    def _():
        _gla_group(lqkf, lvf, lmf, lqkb, lvb, lmb, olf, olb, sf_scr, sb_scr, w2, ba)


def _col_specs(width, col, ngroups):
    blk = col // width
    ctx = pl.BlockSpec((None, CTX_LEN, width), lambda b, s: (b, 0, blk))
    fwd = pl.BlockSpec((None, GROUP, width), lambda b, s: (b, jnp.maximum(s - 1, 0), blk))
    bwd = pl.BlockSpec((None, GROUP, width), lambda b, s: (b, ngroups - 1 - jnp.maximum(s - 1, 0), blk))
    return ctx, fwd, bwd


def _gla_scan(p_c, misc_c, p_l, misc_l, w2, ba):
    nb, seq = p_l.shape[0], p_l.shape[1]
    ng = seq // GROUP
    cqk, fqk, bqk = _col_specs(256, C_GQK, ng)
    cv, fv, bv = _col_specs(256, C_GV, ng)
    cm, fm, bm = _scan_specs(LANES, ng)
    c256, f256, b256 = _scan_specs(256, ng)
    const = lambda shape: pl.BlockSpec(shape, lambda b, s: (0, 0))
    return pl.pallas_call(
        _gla_kernel,
        out_shape=(jax.ShapeDtypeStruct((nb, CTX_LEN, 256), F32), jax.ShapeDtypeStruct((nb, CTX_LEN, 256), F32),
                   jax.ShapeDtypeStruct((nb, seq, 256), F32), jax.ShapeDtypeStruct((nb, seq, 256), F32)),
        grid=(nb, ng + 1),
        in_specs=[cqk, cv, cm, fqk, fv, fm, bqk, bv, bm, const((LANES, 256)), const((1, 256))],
        out_specs=(c256, c256, f256, b256),
        scratch_shapes=[pltpu.VMEM((128, 256), F32), pltpu.VMEM((128, 256), F32)],
        compiler_params=_params(("parallel", "arbitrary")),
        name="gla_scan",
    )(p_c, p_c, misc_c, p_l, p_l, misc_l, p_l, p_l, misc_l, w2, ba)


def _lane_group(width, group):
    return _iota((1, width), 1) // group


def _softmax_parts(parts):
    m = parts[0].max(axis=-1, keepdims=True)
    for s in parts[1:]:
        m = jnp.maximum(m, s.max(axis=-1, keepdims=True))
    ps = [jnp.exp2((s - m).astype(BF16)) for s in parts]
    total = ps[0].astype(F32).sum(axis=-1, keepdims=True)
    for p in ps[1:]:
        total = total + p.astype(F32).sum(axis=-1, keepdims=True)
    return ps, 1.0 / total


def _na_kernel(q_ref, kl_ref, vl_ref, kc_ref, vc_ref, bias_ref, o_ref, *, grid_rows):
    blk = pl.program_id(1)
    key_row0 = jnp.clip(blk * NA_QROWS - NA_WIN_ROWS // 2, 0, grid_rows - NA_KROWS)
    start = pl.multiple_of(key_row0 * GRID_W, GRID_W)
    kw = kl_ref[pl.ds(start, NA_KROWS * GRID_W), :]
    vw = vl_ref[pl.ds(start, NA_KROWS * GRID_W), :]
    kc = kc_ref[...]
    vc = vc_ref[...]
    q = q_ref[...]
    head = _lane_group(256, NA_DH)
    acc = jnp.zeros(q.shape, F32)
    for h in range(N_HEADS):
        qh = jnp.where(head == h, q, jnp.zeros_like(q))
        (p_w, p_c), inv = _softmax_parts([_dot_nt(qh, kw) + bias_ref[h], _dot_nt(qh, kc)])
        o_h = (_dot(p_w.astype(BF16), vw) + _dot(p_c.astype(BF16), vc)) * inv
        acc = acc + jnp.where(head == h, o_h, 0.0)
    o_ref[...] = acc.astype(BF16)


def _na_attention(qkv_l, qkv_c, bias):
    nb, seq = qkv_l.shape[0], qkv_l.shape[1]
    tq = NA_QROWS * GRID_W
    nblk = seq // tq
    variant = lambda j: jnp.where(j == 0, 0, jnp.where(j == nblk - 1, 2, 1))
    kern = functools.partial(_na_kernel, grid_rows=seq // GRID_W)
    return pl.pallas_call(
        kern,
        out_shape=jax.ShapeDtypeStruct((nb, seq, 256), BF16),
        grid=(nb, nblk),
        in_specs=[pl.BlockSpec((None, tq, 256), lambda b, j: (b, j, 0)),
                  pl.BlockSpec((None, seq, 256), lambda b, j: (b, 0, 1)),
                  pl.BlockSpec((None, seq, 256), lambda b, j: (b, 0, 2)),
                  pl.BlockSpec((None, CTX_LEN, 256), lambda b, j: (b, 0, 1)),
                  pl.BlockSpec((None, CTX_LEN, 256), lambda b, j: (b, 0, 2)),
                  pl.BlockSpec((None, N_HEADS, tq, NA_KROWS * GRID_W), lambda b, j: (variant(j), 0, 0, 0))],
        out_specs=pl.BlockSpec((None, tq, 256), lambda b, j: (b, j, 0)),
        compiler_params=_params(("parallel", "arbitrary")),
        name="na_attention",
    )(qkv_l, qkv_l, qkv_l, qkv_c, qkv_c, bias)


def _na_ctx_kernel(q_ref, k_ref, v_ref, o_ref):
    q = q_ref[...]
    k = k_ref[...]
    v = v_ref[...]
    head = _lane_group(256, NA_DH)
    acc = jnp.zeros(q.shape, F32)
    for h in range(N_HEADS):
        qh = jnp.where(head == h, q, jnp.zeros_like(q))
        (p,), inv = _softmax_parts([_dot_nt(qh, k)])
        acc = acc + jnp.where(head == h, _dot(p.astype(BF16), v) * inv, 0.0)
    o_ref[...] = acc.astype(BF16)


def _na_ctx_attention(qkv_c):
    nb = qkv_c.shape[0]
    spec = lambda c: pl.BlockSpec((None, CTX_LEN, 256), lambda b: (b, 0, c))
    return pl.pallas_call(
        _na_ctx_kernel,
        out_shape=jax.ShapeDtypeStruct((nb, CTX_LEN, 256), BF16),
        grid=(nb,),
        in_specs=[spec(0), spec(1), spec(2)],
        out_specs=spec(0),
        compiler_params=_params(("parallel",)),
        name="na_ctx_attention",
    )(qkv_c, qkv_c, qkv_c)


def _na_bias_tables(rpb, grid_rows):
    a = np.arange(NA_QROWS)[:, None, None, None]
    qc = np.arange(GRID_W)[None, :, None, None]
    b = np.arange(NA_KROWS)[None, None, :, None]
    kc = np.arange(GRID_W)[None, None, None, :]
    shape = (NA_QROWS, GRID_W, NA_KROWS, GRID_W)
    half = NA_WIN_ROWS // 2
    n_r, n_c = 2 * NA_WIN_ROWS - 1, 2 * NA_WIN_COLS - 1
    cs = np.clip(qc - NA_WIN_COLS // 2, 0, GRID_W - NA_WIN_COLS)
    ci = np.clip(kc - qc + NA_WIN_COLS - 1, 0, n_c - 1)[0, :, 0, :]
    pick_c = (ci[..., None] == np.arange(n_c)).astype(np.float32)
    vis, pick_r = [], []
    for q_row0 in (0, half, grid_rows - NA_QROWS):
        k_row0 = int(np.clip(q_row0 - half, 0, grid_rows - NA_KROWS))
        qr = q_row0 + a
        kr = k_row0 + b
        rs = np.clip(qr - half, 0, grid_rows - NA_WIN_ROWS)
        vis.append(np.broadcast_to((kr >= rs) & (kr < rs + NA_WIN_ROWS) & (kc >= cs) & (kc < cs + NA_WIN_COLS), shape))
        ri = np.clip(kr - qr + NA_WIN_ROWS - 1, 0, n_r - 1)[:, 0, :, 0]
        pick_r.append((ri[..., None] == np.arange(n_r)).astype(np.float32))
    t = jnp.einsum('hrc,vabr,qkc->vhaqbk', rpb, jnp.asarray(np.stack(pick_r)), jnp.asarray(pick_c), precision=HI)
    t = jnp.where(jnp.asarray(np.stack(vis))[:, None], t * LOG2E, NEG)
    return t.reshape(3, N_HEADS, NA_QROWS * GRID_W, NA_KROWS * GRID_W)


DA_KEY_BLOCK = 128
DA_SCORE_GROUP = 4
DA_VROWS = 64 + 16


def _slab_reduce(x, op, slab=64):
    folded = op(x.reshape(x.shape[0] // slab, slab, x.shape[1]), axis=0)
    return op(folded, axis=0, keepdims=True)


def _da_kernel(*refs, lam_init, has_lat):
    if has_lat:
        q_ref, kl_ref, vtl_ref, kc_ref, vtc_ref, lam_ref, g_ref, o_ref, st_scr, p_scr = refs
    else:
        q_ref, kc_ref, vtc_ref, lam_ref, g_ref, o_ref = refs
    q = q_ref[...]
    tq = q.shape[0]
    lp = lam_ref[...]
    lam = (jnp.exp(jnp.sum(lp[0:1] * lp[1:2], axis=1, keepdims=True))
           - jnp.exp(jnp.sum(lp[2:3] * lp[3:4], axis=1, keepdims=True)) + lam_init)
    qmap = _lane_group(256, DIFF_DQK)
    zero = jnp.zeros_like(q)
    heads = range(N_HEADS)
    hs = [slice(h * 64, (h + 1) * 64) for h in heads]
    vrows = [slice(h * DA_VROWS, (h + 1) * DA_VROWS) for h in heads]
    qs = [jnp.concatenate([jnp.where(qmap == 2 * h, q, zero), jnp.where(qmap == 2 * h + 1, q, zero)], axis=0)
          for h in heads]
    kc = kc_ref[...]
    st = [_dot_nt(kc, qs[h]) for h in heads]
    m = [_slab_reduce(st[h], jnp.max) for h in heads]
    acc = [_dot(vtc_ref[vrows[h], :], jnp.exp2((st[h] - m[h]).astype(BF16))) for h in heads]
    if has_lat:
        n_blocks = kl_ref.shape[0] // DA_KEY_BLOCK

        def block(i):
            start = i * DA_KEY_BLOCK
            return pl.ds(start if isinstance(i, int) else pl.multiple_of(start, DA_KEY_BLOCK), DA_KEY_BLOCK)

        group = DA_SCORE_GROUP
        n_slots = 2 * group

        def scores_group(s):
            kb = kl_ref[pl.ds(group * s * DA_KEY_BLOCK, group * DA_KEY_BLOCK), :]
            for h in heads:
                st = _dot_nt(kb, qs[h])
                for j in range(group):
                    st_scr[(group * s + j) % n_slots, h] = st[j * DA_KEY_BLOCK:(j + 1) * DA_KEY_BLOCK]

        def softmax_stage(s_slot, slot, m):
            m_new = [jnp.maximum(m[h], _slab_reduce(st_scr[s_slot, h], jnp.max)) for h in heads]
            alpha = [jnp.exp2(m[h] - m_new[h]) for h in heads]
            for h in heads:
                p_scr[slot, h] = jnp.exp2((st_scr[s_slot, h] - m_new[h]).astype(BF16))
            return m_new, alpha

        def value_stage(acc, alpha, slot, i):
            return [alpha[h] * acc[h] + _dot(vtl_ref[vrows[h], block(i)], p_scr[slot, h]) for h in heads]

        scores_group(0)
        alpha = None
        for i in range(n_blocks):
            slot = i % 2
            if i % group == 0 and i + group < n_blocks:
                scores_group(i // group + 1)
            m, alpha_i = softmax_stage(i % n_slots, slot, m)
            if i > 0:
                acc = value_stage(acc, alpha, 1 - slot, i - 1)
            alpha = alpha_i
        acc = value_stage(acc, alpha, (n_blocks - 1) % 2, n_blocks - 1)
    outs = []
    for h in heads:
        den = acc[h][64:65, :]
        o_h = acc[h][0:64, 0:tq] * (1.0 / den[:, 0:tq]) - acc[h][0:64, tq:2 * tq] * (lam / den[:, tq:2 * tq])
        y_h = o_h * lax.rsqrt(jnp.mean(o_h * o_h, axis=0, keepdims=True) + EPS) * g_ref[hs[h], :]
        outs.append(y_h * (1.0 - lam_init))
    o_ref[...] = jnp.concatenate(outs, axis=0).T.astype(BF16)


def _da_attention(qkv_q, qkv_l, vt_l, qkv_c, vt_c, lam_params, gain_col, lam_init, tq):
    nb, nq = qkv_q.shape[0], qkv_q.shape[1]
    has_lat = qkv_l is not None
    full = lambda arr, c: pl.BlockSpec((None, arr.shape[1], 256), lambda b, j: (b, 0, c))
    full_t = lambda arr: pl.BlockSpec((None, arr.shape[1], arr.shape[2]), lambda b, j: (b, 0, 0))
    in_specs = [pl.BlockSpec((None, tq, 256), lambda b, j: (b, j, 0))]
    args = [qkv_q]
    if has_lat:
        in_specs += [full(qkv_l, 1), full_t(vt_l)]
        args += [qkv_l, vt_l]
    in_specs += [full(qkv_c, 1), full_t(vt_c),
                 pl.BlockSpec((4, DIFF_DQK), lambda b, j: (0, 0)), pl.BlockSpec((256, 1), lambda b, j: (0, 0))]
    args += [qkv_c, vt_c, lam_params, gain_col]
    scratch = []
    if has_lat:
        assert qkv_l.shape[1] % (DA_KEY_BLOCK * DA_SCORE_GROUP) == 0
        scratch = [pltpu.VMEM((2 * DA_SCORE_GROUP, N_HEADS, DA_KEY_BLOCK, 2 * tq), F32),
                   pltpu.VMEM((2, N_HEADS, DA_KEY_BLOCK, 2 * tq), BF16)]
    return pl.pallas_call(
        functools.partial(_da_kernel, lam_init=lam_init, has_lat=has_lat),
        out_shape=jax.ShapeDtypeStruct((nb, nq, 256), BF16),
        grid=(nb, nq // tq),
        in_specs=in_specs,
        out_specs=pl.BlockSpec((None, tq, 256), lambda b, j: (b, j, 0)),
        scratch_shapes=scratch,
        compiler_params=_params(("parallel", "arbitrary")),
        name="diff_attention",
    )(*args)


def _merge_kernel(x_ref, g1_ref, dnf_ref, dnb_ref, z_ref, glf_ref, glb_ref, r_ref, na_ref, df_ref, gate_ref,
                  bgate_ref, dng_ref, glg_ref, wb_ref, wo_ref, o_ref):
    def head_norm(o, gate, g):
        return ((o * lax.rsqrt(_group_mean(o * o, 64) + EPS) * g) * _silu(gate.astype(F32))).astype(BF16)

    ys = (head_norm(dnf_ref[...] + dnb_ref[...], z_ref[...], dng_ref[...]), na_ref[...],
          head_norm(glf_ref[...] + glb_ref[...], r_ref[...], glg_ref[...]), df_ref[...])
    acc = jnp.zeros(x_ref.shape, F32)
    for g in range(N_BRANCH):
        gate = _sigmoid(gate_ref[:, g * D_MODEL:(g + 1) * D_MODEL].astype(F32) + bgate_ref[g:g + 1, :])
        acc = acc + gate * _dot(ys[g], wb_ref[g])
    o_ref[...] = x_ref[...] + g1_ref[...] * _dot(acc.astype(BF16), wo_ref[...])


def _merge(x2, g1, dn_f, dn_b, gl_f, gl_b, y_na, y_df, p, b_gate, dn_g, gl_g, w_branch, w_out, li, rows_per_mod, tm):
    rows = x2.shape[0]
    row = lambda w, c=0: pl.BlockSpec((tm, w), lambda i: (i, c))
    const = lambda shape: pl.BlockSpec(shape, lambda i: (0,) * len(shape))
    layer = lambda shape: pl.BlockSpec((None,) + shape, lambda i: (li,) + (0,) * len(shape))
    return pl.pallas_call(
        _merge_kernel,
        out_shape=jax.ShapeDtypeStruct((rows, D_MODEL), F32),
        grid=(rows // tm,),
        in_specs=[row(D_MODEL),
                  pl.BlockSpec((None, 1, D_MODEL), lambda i: ((i * tm) // rows_per_mod, 0, 0)),
                  row(256), row(256), row(256, C_DNZ // 256),
                  row(256), row(256), row(256, C_GR // 256),
                  row(256), row(256), row(N_BRANCH * D_MODEL, C_GATE // (N_BRANCH * D_MODEL)),
                  const((N_BRANCH, D_MODEL)), const((1, 256)), const((1, 256)),
                  layer((N_BRANCH, BRANCH_W, D_MODEL)), layer((D_MODEL, D_MODEL))],
        out_specs=row(D_MODEL),
        compiler_params=_params(("parallel",)),
        name="merge",
    )(x2, g1, dn_f, dn_b, p, gl_f, gl_b, p, y_na, y_df, p, b_gate, dn_g, gl_g, w_branch, w_out)


FFN_COLS = 256


def _ffn_kernel(x_ref, xp_ref, xn_ref, gain_ref, sc_ref, sh_ref, g2_ref, wu_ref, wv_ref, cw_ref, cb_ref, wo_ref,
                o_ref, u_scr, act_scr, *, tm, seq_len):
    i = pl.program_id(0)
    first = (i * tm) % seq_len == 0
    last = ((i + 1) * tm) % seq_len == 0

    def norm(x):
        y = x * lax.rsqrt(jnp.mean(x * x, axis=-1, keepdims=True) + EPS)
        return (y * gain_ref[...]) * sc_ref[...] + sh_ref[...]

    x = x_ref[...]
    h = norm(x)
    he = jnp.concatenate([norm(xp_ref[...]), h, norm(xn_ref[...])], axis=0).astype(BF16)
    h = h.astype(BF16)
    rows = _iota((tm + 2 * SUBLANES, 1), 0)
    keep = jnp.logical_not((first & (rows < SUBLANES)) | (last & (rows >= tm + SUBLANES)))
    u_scr[...] = jnp.where(keep, _dot(he, wu_ref[...]), 0.0)
    for c in range(D_FF // FFN_COLS):
        cols = slice(c * FFN_COLS, (c + 1) * FFN_COLS)
        conv =(u_scr[SUBLANES - 1:SUBLANES - 1 + tm, cols] * cw_ref[0:1, cols]
                + u_scr[SUBLANES:SUBLANES + tm, cols] * cw_ref[1:2, cols]
                + u_scr[SUBLANES + 1:SUBLANES + 1 + tm, cols] * cw_ref[2:3, cols]) + cb_ref[:, cols]
        act_scr[:, cols] = (_silu(conv) * _dot(h, wv_ref[:, cols])).astype(BF16)
    o_ref[...] = x + g2_ref[...] * _dot(act_scr[...], wo_ref[...])


def _ffn(x2, gain, scale1p, shift, g2, w_u, w_v, conv_w, conv_b, w_o, li, rows_per_mod, seq_len, tm):
    rows = x2.shape[0]
    nb8 = rows // SUBLANES
    t8 = tm // SUBLANES
    const = lambda shape: pl.BlockSpec(shape, lambda i: (0,) * len(shape))
    weight = lambda shape: pl.BlockSpec((None,) + shape, lambda i: (li,) + (0,) * len(shape),
                                        pipeline_mode=pl.Buffered(1))
    mod = pl.BlockSpec((None, 1, D_MODEL), lambda i: ((i * tm) // rows_per_mod, 0, 0))
    return pl.pallas_call(
        functools.partial(_ffn_kernel, tm=tm, seq_len=seq_len),
        out_shape=jax.ShapeDtypeStruct((rows, D_MODEL), F32),
        grid=(rows // tm,),
        in_specs=[pl.BlockSpec((tm, D_MODEL), lambda i: (i, 0)),
                  pl.BlockSpec((SUBLANES, D_MODEL), lambda i: (jnp.maximum(i * t8 - 1, 0), 0)),
                  pl.BlockSpec((SUBLANES, D_MODEL), lambda i: (jnp.minimum((i + 1) * t8, nb8 - 1), 0)),
                  const((1, D_MODEL)), mod, mod, mod,
                  weight((D_MODEL, D_FF)), weight((D_MODEL, D_FF)), const((3, D_FF)), const((1, D_FF)),
                  weight((D_FF, D_MODEL))],
        out_specs=pl.BlockSpec((tm, D_MODEL), lambda i: (i, 0)),
        scratch_shapes=[pltpu.VMEM((tm + 2 * SUBLANES, D_FF), F32), pltpu.VMEM((tm, D_FF), BF16)],
        compiler_params=_params(("parallel",)),
        name="conv_ffn",
    )(x2, x2, x2, gain, scale1p, shift, g2, w_u, w_v, conv_w, conv_b, w_o)


def _rotate_half_perm():
    quarter = DIFF_DQK // 4
    return np.concatenate([np.arange(quarter, 2 * quarter), np.arange(0, quarter),
                           np.arange(3 * quarter, 4 * quarter), np.arange(2 * quarter, 3 * quarter)])


def _inproj_weights(w_in):
    w = w_in.astype(BF16)
    widths = (768, 256, 8, 8, 768, 128, 128, 256, 256, 32, 768, 4096)
    o = np.concatenate([[0], np.cumsum(widths)])
    seg = lambda a, b: w[:, :, a:b]
    zeros = lambda n: jnp.zeros(w.shape[:2] + (n,), BF16)

    def rot(t):
        t6 = t.reshape(t.shape[:2] + (256 // DIFF_DQK, 2, 2, DIFF_DQK // 4))
        sign = jnp.asarray([-1.0, 1.0], BF16).reshape(1, 1, 1, 1, 2, 1)
        return (jnp.flip(t6, axis=4) * sign).reshape(t.shape)

    pieces = [seg(o[0], o[1]), seg(o[4], o[5]), seg(o[10], o[11]), seg(o[1], o[2]),
              rot(seg(o[10], o[10] + 256)), rot(seg(o[10] + 256, o[10] + 512)),
              seg(o[5], o[7]), seg(o[7], o[8]), seg(o[8], o[9]),
              seg(o[2], o[4]), seg(o[9], o[10]), zeros(C_GATE - C_MISC - 48), seg(o[11], o[12])]
    return jnp.concatenate(pieces, axis=-1)


def _rope_tables(length):
    t = jnp.arange(length)
    row = (t // GRID_W).astype(F32)
    col = (t % GRID_W).astype(F32)
    n_freq = DIFF_DQK // 4
    inv_freq = jnp.power(jnp.float32(ROPE_THETA), -jnp.arange(n_freq, dtype=F32) / n_freq)
    ang_r = row[:, None] * inv_freq
    ang_c = col[:, None] * inv_freq
    ang = jnp.concatenate([ang_r, ang_r, ang_c, ang_c], axis=-1)
    reps = 256 // DIFF_DQK
    return jnp.tile(jnp.cos(ang), (1, reps)), jnp.tile(jnp.sin(ang), (1, reps))


def _pad_lanes(v, offset, width):
    return jnp.zeros((1, width), F32).at[0, offset:offset + v.shape[0]].set(v)


def kernel(x, c, ctx, c_ctx, w_mod, b_mod, norm1_g, norm2_g, w_in, b_gate, dn_conv, dn_a_log, dn_dt_bias, dn_norm_g,
           na_q_norm, na_k_norm, na_rpb, gla_w_a2, gla_b_a, gla_norm_g, df_q_norm, df_k_norm, df_lambda, df_norm_g,
           w_branch, w_out, ffn_w_in, ffn_conv_w, ffn_conv_b, ffn_w_out):
    nb, seq, _ = x.shape
    n_lat, n_ctx = nb * seq, nb * CTX_LEN
    tm_lat, tm_ctx = ROWS_LAT, ROWS_CTX

    cc = jnp.zeros((16, D_MODEL), F32).at[0:nb].set(c).at[nb].set(c_ctx)
    mod = _modulation(cc, w_mod, b_mod)

    perm = _rotate_half_perm()
    w_in_p = _inproj_weights(w_in)
    w_branch_b = w_branch.astype(BF16)
    w_out_b = w_out.astype(BF16)
    w_u = ffn_w_in[:, :, 0:D_FF].astype(BF16)
    w_v = ffn_w_in[:, :, D_FF:2 * D_FF].astype(BF16)
    w_o = ffn_w_out.astype(BF16)
    cos, sin = _rope_tables(seq)

    x2 = x.reshape(n_lat, D_MODEL)
    xc2 = ctx.reshape(n_ctx, D_MODEL)
    for li in range(DEPTH):
        with_ctx = li < DEPTH - 1
        lam_init = 0.8 - 0.6 * math.exp(-0.3 * li)
        m = mod[li]
        part = lambda k: m[:, k * D_MODEL:(k + 1) * D_MODEL]
        lat = lambda v: v[0:nb].reshape(nb, 1, D_MODEL)
        cx = lambda v: v[nb:nb + 1].reshape(1, 1, D_MODEL)
        sh1, sc1, g1, sh2, sc2, g2 = (part(k) for k in range(6))
        n1 = norm1_g[li].reshape(1, D_MODEL)
        n2 = norm2_g[li].reshape(1, D_MODEL)

        p_l, misc_l = _inproj(x2, n1, lat(1.0 + sc1), lat(sh1), w_in_p, li, seq, INPROJ_ROWS)
        p_c, misc_c = _inproj(xc2, n1, cx(1.0 + sc1), cx(sh1), w_in_p, li, n_ctx, INPROJ_ROWS)

        tile4 = lambda v: jnp.tile(v, 256 // v.shape[0]).reshape(1, 256)
        vecs = (dn_conv[li], _pad_lanes(dn_a_log[li].reshape(-1), 2 * N_HEADS, LANES),
                _pad_lanes(dn_dt_bias[li].reshape(-1), 2 * N_HEADS, LANES),
                tile4(na_q_norm[li]), tile4(na_k_norm[li]), tile4(df_q_norm[li]), tile4(df_k_norm[li]),
                tile4(df_q_norm[li][perm]), tile4(df_k_norm[li][perm]))
        dn_l, bg_l, na_l, df_l, vt_l = _prep(p_l, misc_l, cos, sin, vecs, tp=tm_lat, seq_len=seq, rope=True)
        dn_c, bg_c, na_c, df_c, vt_c = _prep(p_c, misc_c, cos, sin, vecs, tp=tm_ctx, seq_len=CTX_LEN, rope=False)

        b3 = lambda a, t: a.reshape(nb, t, a.shape[-1])
        dn_cf, dn_cb, dn_lf, dn_lb = _dn_scan(b3(dn_c, CTX_LEN), b3(bg_c, CTX_LEN), b3(dn_l, seq), b3(bg_l, seq))

        w2 = jnp.zeros((LANES, 256), F32)
        a1_off = 4 * N_HEADS
        w2 = w2.at[a1_off:a1_off + GLA_RANK, 0:128].set(gla_w_a2[li, 0])
        w2 = w2.at[a1_off + GLA_RANK:a1_off + 2 * GLA_RANK, 128:256].set(gla_w_a2[li, 1])
        gl_cf, gl_cb, gl_lf, gl_lb = _gla_scan(b3(p_c, CTX_LEN), b3(misc_c, CTX_LEN), b3(p_l, seq), b3(misc_l, seq),
                                               w2, gla_b_a[li].reshape(1, 256))

        y_na = _na_attention(b3(na_l, seq), b3(na_c, CTX_LEN), _na_bias_tables(na_rpb[li], seq // GRID_W))
        df_gain = tile4(df_norm_g[li]).reshape(256, 1)
        y_df = _da_attention(b3(df_l, seq), b3(df_l, seq), vt_l, b3(df_c, CTX_LEN), vt_c, df_lambda[li],
                             df_gain, lam_init, DA_QUERIES_LAT)

        flat = lambda a: a.reshape(-1, a.shape[-1])
        gains = (b_gate[li], tile4(dn_norm_g[li]), tile4(gla_norm_g[li]), w_branch_b, w_out_b, li)
        ffn_w = (w_u, w_v, ffn_conv_w[li], ffn_conv_b[li].reshape(1, D_FF), w_o, li)
        x2 = _merge(x2, lat(g1), flat(dn_lf), flat(dn_lb), flat(gl_lf), flat(gl_lb), flat(y_na), flat(y_df), p_l,
                    *gains, seq, tm_lat)
        x2 = _ffn(x2, n2, lat(1.0 + sc2), lat(sh2), lat(g2), *ffn_w, seq, seq, tm_lat)
        if with_ctx:
            yc_na = _na_ctx_attention(b3(na_c, CTX_LEN))
            yc_df = _da_attention(b3(df_c, CTX_LEN), None, None, b3(df_c, CTX_LEN), vt_c, df_lambda[li],
                                  df_gain, lam_init, DA_QUERIES_CTX)
            xc2 = _merge(xc2, cx(g1), flat(dn_cf), flat(dn_cb), flat(gl_cf), flat(gl_cb), flat(yc_na), flat(yc_df),
                         p_c, *gains, n_ctx, tm_ctx)
            xc2 = _ffn(xc2, n2, cx(1.0 + sc2), cx(sh2), cx(g2), *ffn_w, n_ctx, CTX_LEN, tm_ctx)
    return x2.reshape(nb, seq, D_MODEL)
```
